```python
import jax
import jax.numpy as jnp
from jax import lax
import numpy as np

D_MODEL = 1024
BATCH = 8
SEQ = 4096
DEPTH = 2

GRID_W = 64
CTX_LEN = 256
N_EVEN = (DEPTH + 1) // 2
N_ODD = DEPTH // 2
NORM_EPS = 1e-6

CONV_WIDTH = D_MODEL // 2
CONV_TAPS = 3
RWKV_HEAD = 64
RWKV_HEADS = (D_MODEL // 2) // RWKV_HEAD
RWKV_WIDTH = RWKV_HEADS * RWKV_HEAD
DECAY_LORA = 64
ICLR_LORA = 64
RWKV_GN_EPS = 64e-5
A_COLS = 4 * CONV_WIDTH
RWKV_SHIFT_COLS = 3 * RWKV_WIDTH + DECAY_LORA + ICLR_LORA
RWKV_SPLITS = (RWKV_WIDTH, 2 * RWKV_WIDTH, 3 * RWKV_WIDTH, 3 * RWKV_WIDTH + DECAY_LORA)
EV_PROJ = A_COLS + RWKV_SHIFT_COLS + RWKV_WIDTH
EV_MIX = CONV_WIDTH + RWKV_WIDTH

MLA_HEADS = 8
QK_NOPE = 128
QK_ROPE = 64
V_HEAD = 128
Q_LORA = 384
KV_LORA = 256
MLA_WIDTH = MLA_HEADS * V_HEAD
Z_OFF = Q_LORA + KV_LORA + QK_ROPE
OD_PROJ = Z_OFF + MLA_WIDTH
SM_SCALE = (QK_NOPE + QK_ROPE) ** -0.5
ROPE_THETA = 10000.0
ROPE_PAIRS = QK_ROPE // 4
Q_BLOCK = 128

kernel_name = 'hybrid_conv_rwkv7_mla_prefix_dit'


def rms_norm(x, w):
    xf = x.astype(jnp.float32)
    y = xf * lax.rsqrt(jnp.mean(xf * xf, axis=-1, keepdims=True) + NORM_EPS)
    return (y * w.astype(jnp.float32)).astype(x.dtype)


def shift_prev(u):
    return jnp.pad(u, ((0, 0), (1, 0), (0, 0)))[:, :-1]


def shift_next(u):
    return jnp.pad(u, ((0, 0), (0, 1), (0, 0)))[:, 1:]


def to_heads(t):
    return t.reshape(t.shape[:-1] + (RWKV_HEADS, RWKV_HEAD))


def axial_rope_tables(n):
    rows = n // GRID_W
    row = jnp.repeat(jnp.arange(rows, dtype=jnp.float32), GRID_W)
    col = jnp.tile(jnp.arange(GRID_W, dtype=jnp.float32), rows)
    inv_freq = ROPE_THETA ** (-jnp.arange(ROPE_PAIRS, dtype=jnp.float32) / ROPE_PAIRS)
    ang = jnp.stack([row[:, None] * inv_freq, col[:, None] * inv_freq], axis=1)
    return jnp.cos(ang), jnp.sin(ang)


def apply_axial_rope(t, cos, sin):
    tf = t.astype(jnp.float32).reshape(t.shape[:-1] + (2, 2, ROPE_PAIRS))
    t1, t2 = tf[..., 0, :], tf[..., 1, :]
    out = jnp.stack([t1 * cos - t2 * sin, t2 * cos + t1 * sin], axis=-2)
    return out.reshape(t.shape).astype(t.dtype)


def short_conv_branch(pa, conv_w):
    u, gate_b, gate_c, z = jnp.split(pa, 4, axis=-1)
    cu = gate_c * u
    y = conv_w[0] * shift_prev(cu) + conv_w[1] * cu + conv_w[2] * shift_next(cu)
    return gate_b * y * jax.nn.silu(z)


def rwkv_prep(pr, mu, k_k):
    pr = pr.astype(jnp.float32)
    pr = pr + mu * (0.5 * (shift_prev(pr) + shift_next(pr)) - pr)
    r, k, v, wl, al = jnp.split(pr, RWKV_SPLITS, axis=-1)
    kk = to_heads(k * k_k)
    kk = kk / jnp.maximum(jnp.linalg.norm(kk, axis=-1, keepdims=True), 1e-12)
    return to_heads(r), k, to_heads(v), wl, al, kk


def rwkv_direction(k, wl, al, kk, w0, w2, a0, a2, k_a):
    w_log = -jax.nn.softplus(-(w0 + jnp.tanh(wl) @ w2)) - 0.5
    decay = jnp.exp(-jnp.exp(w_log))
    iclr = jax.nn.sigmoid(a0 + al @ a2)
    k_dir = k * (1.0 + (iclr - 1.0) * k_a)
    return to_heads(decay), to_heads(k_dir), kk * to_heads(iclr)


def rwkv7_scan(r, decay, k, v, a_vec, b_vec, s0, reverse):
    def step(S, inp):
        r_t, w_t, k_t, v_t, a_t, b_t = inp
        sa = jnp.einsum('bhij,bhj->bhi', S, a_t)
        S = S * w_t[:, :, None, :] + sa[..., None] * b_t[:, :, None, :] + v_t[..., None] * k_t[:, :, None, :]
        return S, jnp.einsum('bhij,bhj->bhi', S, r_t)
    xs = tuple(jnp.moveaxis(t, 1, 0) for t in (r, decay, k, v, a_vec, b_vec))
    S, ys = lax.scan(step, s0, xs, reverse=reverse)
    return jnp.moveaxis(ys, 0, 1), S


def head_group_norm(y, w, b):
    mean = jnp.mean(y, axis=-1, keepdims=True)
    var = jnp.mean(jnp.square(y - mean), axis=-1, keepdims=True)
    yn = (y - mean) * lax.rsqrt(var + RWKV_GN_EPS)
    return yn * w.reshape(RWKV_HEADS, RWKV_HEAD) + b.reshape(RWKV_HEADS, RWKV_HEAD)


def even_output(p, y, bon, conv_w, lnx_w, lnx_b, w_out):
    a_out = short_conv_branch(p[..., :A_COLS], conv_w)
    z_b = p[..., A_COLS + RWKV_SHIFT_COLS:]
    rw = head_group_norm(y, lnx_w, lnx_b) + bon
    b_out = rw.reshape(rw.shape[:2] + (RWKV_WIDTH,)) * jax.nn.silu(z_b.astype(jnp.float32))
    return jnp.concatenate([a_out, b_out.astype(p.dtype)], axis=-1) @ w_out


def even_mixer(h_ctx, h_lat, w_in, conv_w, mu, k_k, k_a, w0, w2, a0, a2, r_k, lnx_w, lnx_b, w_out, need_ctx):
    p_all = (h_ctx @ w_in, h_lat @ w_in)
    streams = [rwkv_prep(p[..., A_COLS:A_COLS + RWKV_SHIFT_COLS], mu, k_k) for p in p_all]
    batch = h_lat.shape[0]
    y_sum = [0.0, 0.0]
    bonus = [0.0, 0.0]
    for d, reverse in enumerate((False, True)):
        state = jnp.zeros((batch, RWKV_HEADS, RWKV_HEAD, RWKV_HEAD), jnp.float32)
        for s, (r, k, v, wl, al, kk) in enumerate(streams):
            decay, k_dir, b_vec = rwkv_direction(k, wl, al, kk, w0[d], w2[d], a0[d], a2[d], k_a)
            y, state = rwkv7_scan(r, decay, k_dir, v, -kk, b_vec, state, reverse)
            y_sum[s] = y_sum[s] + y
            bonus[s] = bonus[s] + jnp.sum(r * k_dir * r_k, axis=-1, keepdims=True) * v
    y_lat = even_output(p_all[1], y_sum[1], bonus[1], conv_w, lnx_w, lnx_b, w_out)
    y_ctx = even_output(p_all[0], y_sum[0], bonus[0], conv_w, lnx_w, lnx_b, w_out) if need_ctx else None
    return y_ctx, y_lat


def block_attention(q_nope, q_rope, k_nope, k_rope, v):
    batch, n_q = q_nope.shape[:2]
    n_blk = n_q // Q_BLOCK

    def to_blocks(t):
        return jnp.swapaxes(t.reshape((batch, n_blk, Q_BLOCK) + t.shape[2:]), 0, 1)

    def attend(qs):
        qn, qr = qs
        s = jnp.einsum('bqhd,bkhd->bhqk', qn, k_nope) + jnp.einsum('bqhr,bkr->bhqk', qr, k_rope)
        p = jax.nn.softmax(s.astype(jnp.float32), axis=-1).astype(v.dtype)
        return jnp.einsum('bhqk,bkhd->bqhd', p, v)

    o = lax.map(attend, (to_blocks(q_nope), to_blocks(q_rope)))
    return jnp.swapaxes(o, 0, 1).reshape((batch, n_q) + o.shape[3:])


def odd_mixer(h_ctx, h_lat, w_in, q_a_norm, kv_a_norm, w_qb, w_kvb, gq_nope, gq_rope, gk_nope, gk_rope, w_o, cos, sin, need_ctx):
    def keys_values(kv_in, rotary):
        kv_a, k_rope = kv_in[..., :KV_LORA], kv_in[..., KV_LORA:]
        kv = rms_norm(kv_a, kv_a_norm) @ w_kvb
        kv = kv.reshape(kv.shape[:2] + (MLA_HEADS, QK_NOPE + V_HEAD))
        k_nope = rms_norm(kv[..., :QK_NOPE], gk_nope)
        k_rope = rms_norm(k_rope, gk_rope)
        if rotary:
            k_rope = apply_axial_rope(k_rope, cos, sin)
        return k_nope, k_rope, kv[..., QK_NOPE:]

    def queries(q_a, rotary):
        q = rms_norm(q_a, q_a_norm) @ w_qb
        q = q.reshape(q.shape[:2] + (MLA_HEADS, QK_NOPE + QK_ROPE))
        q_nope = rms_norm(q[..., :QK_NOPE], gq_nope) * SM_SCALE
        q_rope = rms_norm(q[..., QK_NOPE:], gq_rope) * SM_SCALE
        if rotary:
            q_rope = apply_axial_rope(q_rope, cos[:, None], sin[:, None])
        return q_nope, q_rope

    def out(o, z):
        return (o.reshape(o.shape[:2] + (MLA_WIDTH,)) * jax.nn.silu(z)) @ w_o

    kn_c, kr_c, v_c = keys_values(h_ctx @ w_in[:, Q_LORA:Z_OFF], False)
    p = h_lat @ w_in
    kn_l, kr_l, v_l = keys_values(p[..., Q_LORA:Z_OFF], True)
    qn, qr = queries(p[..., :Q_LORA], True)
    k_nope = jnp.concatenate([kn_c, kn_l], axis=1)
    k_rope = jnp.concatenate([kr_c, kr_l], axis=1)
    v = jnp.concatenate([v_c, v_l], axis=1)
    y_lat = out(block_attention(qn, qr, k_nope, k_rope, v), p[..., Z_OFF:])
    y_ctx = None
    if need_ctx:
        qn_c, qr_c = queries(h_ctx @ w_in[:, :Q_LORA], False)
        y_ctx = out(block_attention(qn_c, qr_c, kn_c, kr_c, v_c), h_ctx @ w_in[:, Z_OFF:])
    return y_ctx, y_lat


def setup_inputs(seed: int = 0) -> dict:
    key = jax.random.key(seed)
    ks = iter(jax.random.split(key, 40))

    def nrm(shape, scale):
        return scale * jax.random.normal(next(ks), shape, jnp.float32)

    D = D_MODEL
    frac = jnp.arange(RWKV_WIDTH, dtype=jnp.float32) / (RWKV_WIDTH - 1)
    decay_base = -6.0 + 5.0 * frac ** 0.9
    return {
        'x': nrm((BATCH, SEQ, D), 1.0),
        'c': nrm((BATCH, D), 1.0),
        'ctx': nrm((BATCH, CTX_LEN, D), 1.0),
        'c_ctx': nrm((D,), 1.0),
        'ada_w': nrm((DEPTH, D, 3 * D), 0.5 * D ** -0.5),
        'ada_b': nrm((DEPTH, 3 * D), 0.02),
        'norm_w': 1.0 + nrm((DEPTH, D), 0.02),
        'ev_w_in': nrm((N_EVEN, D, EV_PROJ), D ** -0.5),
        'ev_conv_w': nrm((N_EVEN, CONV_TAPS, CONV_WIDTH), CONV_TAPS ** -0.5),
        'ev_mu': jax.random.uniform(next(ks), (N_EVEN, RWKV_SHIFT_COLS), jnp.float32),
        'ev_k_k': 0.85 + nrm((N_EVEN, RWKV_WIDTH), 0.02),
        'ev_k_a': 1.0 + nrm((N_EVEN, RWKV_WIDTH), 0.02),
        'ev_w0': decay_base + nrm((N_EVEN, 2, RWKV_WIDTH), 0.1),
        'ev_w2': nrm((N_EVEN, 2, DECAY_LORA, RWKV_WIDTH), 0.5 * DECAY_LORA ** -0.5),
        'ev_a0': nrm((N_EVEN, 2, RWKV_WIDTH), 0.1),
        'ev_a2': nrm((N_EVEN, 2, ICLR_LORA, RWKV_WIDTH), 0.5 * ICLR_LORA ** -0.5),
        'ev_r_k': nrm((N_EVEN, RWKV_HEADS, RWKV_HEAD), 0.1),
        'ev_lnx_w': 1.0 + nrm((N_EVEN, RWKV_WIDTH), 0.02),
        'ev_lnx_b': nrm((N_EVEN, RWKV_WIDTH), 0.02),
        'ev_w_out': nrm((N_EVEN, EV_MIX, D), EV_MIX ** -0.5),
        'od_w_in': nrm((N_ODD, D, OD_PROJ), D ** -0.5),
        'od_q_a_norm': 1.0 + nrm((N_ODD, Q_LORA), 0.02),
        'od_kv_a_norm': 1.0 + nrm((N_ODD, KV_LORA), 0.02),
        'od_w_qb': nrm((N_ODD, Q_LORA, MLA_HEADS * (QK_NOPE + QK_ROPE)), Q_LORA ** -0.5),
        'od_w_kvb': nrm((N_ODD, KV_LORA, MLA_HEADS * (QK_NOPE + V_HEAD)), KV_LORA ** -0.5),
        'od_gq_nope': 1.0 + nrm((N_ODD, QK_NOPE), 0.02),
        'od_gq_rope': 1.0 + nrm((N_ODD, QK_ROPE), 0.02),
        'od_gk_nope': 1.0 + nrm((N_ODD, QK_NOPE), 0.02),
        'od_gk_rope': 1.0 + nrm((N_ODD, QK_ROPE), 0.02),
        'od_w_o': nrm((N_ODD, MLA_WIDTH, D), MLA_WIDTH ** -0.5),
    }


def reference(x, c, ctx, c_ctx, ada_w, ada_b, norm_w, ev_w_in, ev_conv_w, ev_mu, ev_k_k, ev_k_a, ev_w0, ev_w2, ev_a0, ev_a2, ev_r_k, ev_lnx_w, ev_lnx_b, ev_w_out, od_w_in, od_q_a_norm, od_kv_a_norm, od_w_qb, od_w_kvb, od_gq_nope, od_gq_rope, od_gk_nope, od_gk_rope, od_w_o):
    cos, sin = axial_rope_tables(x.shape[1])
    silu_c = jax.nn.silu(c)
    silu_cc = jax.nn.silu(c_ctx)
    for layer in range(DEPTH):
        need_ctx = layer < DEPTH - 1
        shift, scale, gate = jnp.split(silu_c @ ada_w[layer] + ada_b[layer], 3, axis=-1)
        shift_c, scale_c, gate_c = jnp.split(silu_cc @ ada_w[layer] + ada_b[layer], 3, axis=-1)
        h_lat = rms_norm(x, norm_w[layer]) * (1.0 + scale[:, None]) + shift[:, None]
        h_ctx = rms_norm(ctx, norm_w[layer]) * (1.0 + scale_c) + shift_c
        j = layer // 2
        if layer % 2 == 0:
            y_ctx, y_lat = even_mixer(h_ctx, h_lat, ev_w_in[j], ev_conv_w[j], ev_mu[j], ev_k_k[j], ev_k_a[j],
                                      ev_w0[j], ev_w2[j], ev_a0[j], ev_a2[j], ev_r_k[j], ev_lnx_w[j], ev_lnx_b[j],
                                      ev_w_out[j], need_ctx)
        else:
            y_ctx, y_lat = odd_mixer(h_ctx, h_lat, od_w_in[j], od_q_a_norm[j], od_kv_a_norm[j], od_w_qb[j],
                                     od_w_kvb[j], od_gq_nope[j], od_gq_rope[j], od_gk_nope[j], od_gk_rope[j],
                                     od_w_o[j], cos, sin, need_ctx)
        x = x + gate[:, None] * y_lat
        if need_ctx:
            ctx = ctx + gate_c * y_ctx
    return x
```

```python
import functools

import numpy as np
import jax
import jax.numpy as jnp
from jax import lax
from jax.experimental import pallas as pl
from jax.experimental.pallas import tpu as pltpu

F32 = jnp.float32
BF16 = jnp.bfloat16
HIGHEST = lax.Precision.HIGHEST

NORM_EPS = 1e-6
GRID_W = 64

CONV_WIDTH = 512
RWKV_HEAD = 64
RWKV_WIDTH = 512
LORA = 64
RWKV_GN_EPS = 64e-5
A_COLS = 4 * CONV_WIDTH
SHIFT_COLS = 3 * RWKV_WIDTH + 2 * LORA
CHUNK = 64
PAIR = 2 * RWKV_HEAD
N_PAIRS = RWKV_WIDTH // PAIR

MLA_HEADS = 8
QK_NOPE = 128
QK_ROPE = 64
QK_DIM = QK_NOPE + QK_ROPE
V_HEAD = 128
Q_LORA = 384
KV_LORA = 256
MLA_WIDTH = MLA_HEADS * V_HEAD
SM_SCALE = QK_DIM ** -0.5
ROPE_THETA = 10000.0
ROPE_PAIRS = QK_ROPE // 4

VMEM_LIMIT_BYTES = 56 * 1024 * 1024
HALO = 8


def _cparams(*sem):
    return pltpu.CompilerParams(dimension_semantics=sem, vmem_limit_bytes=VMEM_LIMIT_BYTES)


def _silu(x):
    return x / (1.0 + jnp.exp(-x))


def _dot(a, b):
    return jnp.dot(a.astype(BF16), b.astype(BF16), preferred_element_type=F32)


def _dot_nt(a, b):
    return lax.dot_general(a.astype(BF16), b.astype(BF16), (((1,), (1,)), ((), ())),
                           preferred_element_type=F32)


def _dot_f32(a, b):
    return jnp.dot(a, b, precision=HIGHEST, preferred_element_type=F32)


def _split_dot(x, e_bf16):
    hi = x.astype(BF16)
    lo = (x - hi.astype(F32)).astype(BF16)
    return (jnp.dot(hi, e_bf16, preferred_element_type=F32)
            + jnp.dot(lo, e_bf16, preferred_element_type=F32))


def _full(shape):
    return pl.BlockSpec(shape, lambda *_: (0,) * len(shape))


def _ada_kernel(c_ref, w_ref, b_ref, o_ref):
    o_ref[0] = _dot_f32(_silu(c_ref[...]), w_ref[0]) + b_ref[0]


def _ada_call(cc, ada_w, ada_b):
    depth, d, d3 = ada_w.shape
    rows = cc.shape[0]
    tn = 1024
    return pl.pallas_call(
        _ada_kernel,
        grid=(depth, d3 // tn),
        in_specs=[pl.BlockSpec((rows, d), lambda l, j: (0, 0)),
                  pl.BlockSpec((1, d, tn), lambda l, j: (l, 0, j)),
                  pl.BlockSpec((1, 1, tn), lambda l, j: (l, 0, j))],
        out_specs=pl.BlockSpec((1, rows, tn), lambda l, j: (l, 0, j)),
        out_shape=jax.ShapeDtypeStruct((depth, rows, d3), F32),
        compiler_params=_cparams("arbitrary", "arbitrary"),
        name="ada_mod",
    )(cc, ada_w, ada_b.reshape(depth, 1, d3))


def _modulated_norm(x, nw, scale, shift):
    rs = lax.rsqrt(jnp.mean(x * x, axis=-1, keepdims=True) + NORM_EPS)
    return (x * rs * nw) * (1.0 + scale) + shift


def _ev_in_kernel(x_ref, sc_ref, sh_ref, nw_ref, w_ref, pa_ref, pr_ref, zb_ref):
    h = _modulated_norm(x_ref[0], nw_ref[...], sc_ref[0], sh_ref[0]).astype(BF16)
    pa_ref[0] = jnp.dot(h, w_ref[:, :A_COLS], preferred_element_type=F32)
    pr_ref[0] = jnp.dot(h, w_ref[:, A_COLS:A_COLS + SHIFT_COLS], preferred_element_type=F32)
    zb_ref[0] = jnp.dot(h, w_ref[:, A_COLS + SHIFT_COLS:], preferred_element_type=F32)


def _ev_in_call(x, scale, shift, nw, w_bf16, tm):
    nb, t, d = x.shape
    ncol = w_bf16.shape[1]
    row = lambda b, i: (b, i, 0)
    vec = lambda b, i: (b, 0, 0)
    return pl.pallas_call(
        _ev_in_kernel,
        grid=(nb, t // tm),
        in_specs=[pl.BlockSpec((1, tm, d), row),
                  pl.BlockSpec((1, 1, d), vec),
                  pl.BlockSpec((1, 1, d), vec),
                  _full((1, d)),
                  _full((d, ncol))],
        out_specs=[pl.BlockSpec((1, tm, A_COLS), row),
                   pl.BlockSpec((1, tm, SHIFT_COLS), row),
                   pl.BlockSpec((1, tm, RWKV_WIDTH), row)],
        out_shape=[jax.ShapeDtypeStruct((nb, t, A_COLS), F32),
                   jax.ShapeDtypeStruct((nb, t, SHIFT_COLS), F32),
                   jax.ShapeDtypeStruct((nb, t, RWKV_WIDTH), F32)],
        compiler_params=_cparams("parallel", "parallel"),
        name="ev_in",
    )(x, scale, shift, nw, w_bf16)


def _neighbours(cur, prev_halo, next_halo, is_first, is_last):
    n = cur.shape[0]
    rowi = lax.broadcasted_iota(jnp.int32, cur.shape, 0)
    prev_row = jnp.where(is_first, 0.0, prev_halo[HALO - 1:HALO, :])
    next_row = jnp.where(is_last, 0.0, next_halo[0:1, :])
    prev = jnp.where(rowi == 0, prev_row, pltpu.roll(cur, 1, 0))
    nxt = jnp.where(rowi == n - 1, next_row, pltpu.roll(cur, n - 1, 0))
    return prev, nxt


def _scan_kernel(reverse, n_chunks, has_s0, *refs):
    refs = list(refs)
    (pr_ref, prv_ref, nxt_ref, mu_ref, kk_ref, ka_ref, rk_ref, w0_ref, a0_ref, lora_ref, e_ref) = refs[:11]
    pos = 11
    s0_ref = None
    if has_s0:
        s0_ref = refs[pos]
        pos += 1
    y_ref, bon_ref, sfin_ref, st_ref = refs[pos:pos + 4]

    C = CHUNK
    c = pl.program_id(1)
    cs = (n_chunks - 1 - c) if reverse else c

    @pl.when(c == 0)
    def _():
        if has_s0:
            st_ref[...] = s0_ref[0]
        else:
            st_ref[...] = jnp.zeros_like(st_ref)

    pr = pr_ref[0]
    prev, nxt = _neighbours(pr, prv_ref[0], nxt_ref[0], cs == 0, cs == n_chunks - 1)
    x = pr + mu_ref[...] * (0.5 * (prev + nxt) - pr)
    W = RWKV_WIDTH
    r = x[:, 0:W]
    k = x[:, W:2 * W]
    v = x[:, 2 * W:3 * W]
    wa = x[:, 3 * W:3 * W + 2 * LORA]

    lane = lax.broadcasted_iota(jnp.int32, (C, PAIR), 1)
    first_head = lane < RWKV_HEAD
    tw = jnp.where(first_head, jnp.tanh(wa), wa)
    lo = _dot_f32(tw, lora_ref[...])
    nz = -(w0_ref[...] + lo[:, :W])
    softplus = jnp.maximum(nz, 0.0) + jnp.log1p(jnp.exp(-jnp.abs(nz)))
    lw = -jnp.exp(-softplus - 0.5)
    iclr = 1.0 / (1.0 + jnp.exp(-(a0_ref[...] + lo[:, W:])))

    e = e_ref[...]
    kkv = k * kk_ref[...]
    kkn = kkv / jnp.maximum(jnp.sqrt(_split_dot(kkv * kkv, e)), 1e-12)
    k_dir = k * (1.0 + (iclr - 1.0) * ka_ref[...])
    bvec = kkn * iclr
    avec = -kkn
    bon_ref[0] = _split_dot(r * k_dir * rk_ref[...], e) * v

    ti = lax.broadcasted_iota(jnp.int32, (C, C), 0)
    si = lax.broadcasted_iota(jnp.int32, (C, C), 1)
    tri = ((si >= ti) if reverse else (si <= ti)).astype(F32)
    g_in = _dot_f32(tri, lw)
    g_tot = jnp.sum(lw, axis=0, keepdims=True)
    gam = jnp.exp(g_in)
    gam_ex = jnp.exp(g_in - lw)
    gam_inv = jnp.exp(-g_in)
    gam_tail = jnp.exp(g_tot - g_in)
    gam_c = jnp.exp(g_tot)
    rh = r * gam
    ah = avec * gam_ex
    bc = bvec * gam_inv
    kc = k_dir * gam_inv
    bt = bvec * gam_tail
    kt = k_dir * gam_tail

    n2 = 2 * C
    ri = lax.broadcasted_iota(jnp.int32, (n2, n2), 0)
    ci = lax.broadcasted_iota(jnp.int32, (n2, n2), 1)
    same_head = (ri >= C) == (ci >= C)
    rt = ri & (C - 1)
    ct = ci & (C - 1)
    if reverse:
        strict = same_head & (ct > rt)
        incl = same_head & (ct >= rt)
    else:
        strict = same_head & (ct < rt)
        incl = same_head & (ct <= rt)
    diag = ri == ci
    eye = diag.astype(F32)
    blocks = [same_head & ((rt >> s) == (ct >> s)) for s in range(1, C.bit_length())]

    def stk(z):
        return jnp.concatenate([jnp.where(first_head, z, 0.0), jnp.where(first_head, 0.0, z)], axis=0)

    def dup(z):
        return jnp.concatenate([z, z], axis=0)

    zeros = jnp.zeros((n2, PAIR), F32)
    for p in range(N_PAIRS):
        sl = slice(p * PAIR, (p + 1) * PAIR)
        ah_s = stk(ah[:, sl])
        rh_s = stk(rh[:, sl])
        v_s = stk(v[:, sl])
        pm = _dot_nt(jnp.concatenate([ah_s, rh_s], axis=0),
                     jnp.concatenate([dup(bc[:, sl]), dup(kc[:, sl])], axis=0))
        a_ab = jnp.where(strict, pm[:n2, :n2], 0.0)
        a_ak = jnp.where(strict, pm[:n2, n2:], 0.0)
        a_rb = jnp.where(incl, pm[n2:, :n2], 0.0)
        a_rk = jnp.where(incl, pm[n2:, n2:], 0.0)
        tinv = jnp.where(blocks[0], a_ab, 0.0) + eye
        for lvl in range(1, len(blocks)):
            a_off = jnp.where(blocks[lvl] & ~blocks[lvl - 1], a_ab, 0.0)
            tinv = tinv + _dot(tinv, _dot(a_off, tinv))
        xx = _dot(tinv, jnp.concatenate([ah_s, _dot(a_ak, v_s)], axis=1))
        rhs = jnp.concatenate([xx, jnp.concatenate([zeros, v_s], axis=1)], axis=0)
        lhs = jnp.concatenate(
            [jnp.concatenate([a_rb, a_rk], axis=1),
             jnp.concatenate([stk(bt[:, sl]).T, stk(kt[:, sl]).T], axis=1)], axis=0)
        rr = _dot(lhs, rhs)
        q_t = rh_s + rr[:n2, :PAIR]
        y2 = rr[:n2, PAIR:]
        m_w = jnp.where(diag, gam_c[:, sl], 0.0) + rr[n2:, :PAIR]
        g_w = rr[n2:, PAIR:]
        ff = _dot_f32(jnp.concatenate([q_t, m_w], axis=0), st_ref[p])
        ys = ff[:n2] + y2
        st_ref[p] = ff[n2:] + g_w
        y_ref[0, :, sl] = ys[:C] + ys[C:]

    @pl.when(c == n_chunks - 1)
    def _():
        sfin_ref[0] = st_ref[...]


def _scan_call(pr, params, s0, reverse):
    nb, t, _ = pr.shape
    C = CHUNK
    n_chunks = t // C
    hb = C // HALO
    n_hblk = t // HALO

    def cs_of(c):
        return (n_chunks - 1 - c) if reverse else c

    row = lambda b, c: (b, cs_of(c), 0)
    prv = lambda b, c: (b, jnp.maximum(cs_of(c) * hb - 1, 0), 0)
    nxt = lambda b, c: (b, jnp.minimum((cs_of(c) + 1) * hb, n_hblk - 1), 0)
    st_spec = pl.BlockSpec((1, N_PAIRS, PAIR, PAIR), lambda b, c: (b, 0, 0, 0))
    in_specs = [pl.BlockSpec((1, C, SHIFT_COLS), row),
                pl.BlockSpec((1, HALO, SHIFT_COLS), prv),
                pl.BlockSpec((1, HALO, SHIFT_COLS), nxt)]
    in_specs += [_full(p.shape) for p in params]
    args = [pr, pr, pr] + list(params)
    if s0 is not None:
        in_specs.append(st_spec)
        args.append(s0)
    return pl.pallas_call(
        functools.partial(_scan_kernel, reverse, n_chunks, s0 is not None),
        grid=(nb, n_chunks),
        in_specs=in_specs,
        out_specs=[pl.BlockSpec((1, C, RWKV_WIDTH), row),
                   pl.BlockSpec((1, C, RWKV_WIDTH), row),
                   st_spec],
        out_shape=[jax.ShapeDtypeStruct((nb, t, RWKV_WIDTH), F32),
                   jax.ShapeDtypeStruct((nb, t, RWKV_WIDTH), F32),
                   jax.ShapeDtypeStruct((nb, N_PAIRS, PAIR, PAIR), F32)],
        scratch_shapes=[pltpu.VMEM((N_PAIRS, PAIR, PAIR), F32)],
        compiler_params=_cparams("parallel", "arbitrary"),
        name="rwkv_scan_rev" if reverse else "rwkv_scan_fwd",
    )(*args)


def _ev_out_kernel(n_tiles, pa_ref, pap_ref, pan_ref, zb_ref, yf_ref, yb_ref, bf_ref, bb_ref,
                   cw_ref, lw_ref, lb_ref, e_ref, wo_ref, x_ref, g_ref, o_ref):
    i = pl.program_id(1)
    W = CONV_WIDTH
    pa = pa_ref[0]
    u, gate_b, gate_c, z = pa[:, 0:W], pa[:, W:2 * W], pa[:, 2 * W:3 * W], pa[:, 3 * W:4 * W]
    cu = gate_c * u
    cu_p = pap_ref[0][:, 2 * W:3 * W] * pap_ref[0][:, 0:W]
    cu_n = pan_ref[0][:, 2 * W:3 * W] * pan_ref[0][:, 0:W]
    prev, nxt = _neighbours(cu, cu_p, cu_n, i == 0, i == n_tiles - 1)
    cw = cw_ref[...]
    conv = cw[0:1] * prev + cw[1:2] * cu + cw[2:3] * nxt
    a_out = gate_b * conv * _silu(z)

    e = e_ref[...]
    ysum = yf_ref[0] + yb_ref[0]
    inv_n = 1.0 / RWKV_HEAD
    mean = _split_dot(ysum, e) * inv_n
    dlt = ysum - mean
    var = _split_dot(dlt * dlt, e) * inv_n
    rw = dlt * lax.rsqrt(var + RWKV_GN_EPS) * lw_ref[...] + lb_ref[...] + (bf_ref[0] + bb_ref[0])
    b_out = rw * _silu(zb_ref[0])

    y = (jnp.dot(a_out.astype(BF16), wo_ref[:W, :], preferred_element_type=F32)
         + jnp.dot(b_out.astype(BF16), wo_ref[W:, :], preferred_element_type=F32))
    o_ref[0] = x_ref[0] + g_ref[0] * y


def _ev_out_call(pa, zb, yf, yb, bf, bb, conv_w, lnx_w, lnx_b, e_bf16, wo_bf16, x, gate, tm):
    nb, t, d = x.shape
    n_tiles = t // tm
    hb = tm // HALO
    n_hblk = t // HALO
    row = lambda b, i: (b, i, 0)
    prv = lambda b, i: (b, jnp.maximum(i * hb - 1, 0), 0)
    nxt = lambda b, i: (b, jnp.minimum((i + 1) * hb, n_hblk - 1), 0)
    vec = lambda b, i: (b, 0, 0)
    wspec = pl.BlockSpec((1, tm, RWKV_WIDTH), row)
    return pl.pallas_call(
        functools.partial(_ev_out_kernel, n_tiles),
        grid=(nb, n_tiles),
        in_specs=[pl.BlockSpec((1, tm, A_COLS), row),
                  pl.BlockSpec((1, HALO, A_COLS), prv),
                  pl.BlockSpec((1, HALO, A_COLS), nxt),
                  wspec, wspec, wspec, wspec, wspec,
                  _full(conv_w.shape), _full(lnx_w.shape), _full(lnx_b.shape),
                  _full(e_bf16.shape), _full(wo_bf16.shape),
                  pl.BlockSpec((1, tm, d), row),
                  pl.BlockSpec((1, 1, d), vec)],
        out_specs=pl.BlockSpec((1, tm, d), row),
        out_shape=jax.ShapeDtypeStruct((nb, t, d), F32),
        compiler_params=_cparams("parallel", "parallel"),
        name="ev_out",
    )(pa, pa, pa, zb, yf, yb, bf, bb, conv_w, lnx_w, lnx_b, e_bf16, wo_bf16, x, gate)


def _head_rms(t, g):
    return t * lax.rsqrt(jnp.mean(t * t, axis=-1, keepdims=True) + NORM_EPS) * g


def _keys_values(p_kv, kvn, wkvb, gkn, gkr, gkr_sw, cos, sin, k_ref, v_ref):
    kv_a = p_kv[:, :KV_LORA]
    kr = p_kv[:, KV_LORA:KV_LORA + QK_ROPE]
    kr_sw = p_kv[:, KV_LORA + QK_ROPE:KV_LORA + 2 * QK_ROPE]
    rs = lax.rsqrt(jnp.mean(kr * kr, axis=-1, keepdims=True) + NORM_EPS)
    if cos is None:
        k_rope = kr * rs * gkr
    else:
        k_rope = ((kr * gkr) * cos + (kr_sw * gkr_sw) * sin) * rs
    k_rope = k_rope.astype(BF16)
    kv = jnp.dot(_head_rms(kv_a, kvn).astype(BF16), wkvb, preferred_element_type=F32)
    for h in range(MLA_HEADS):
        kn = _head_rms(kv[:, h * QK_NOPE:(h + 1) * QK_NOPE], gkn)
        k_ref[0, h, :, 0:QK_NOPE] = kn.astype(BF16)
        k_ref[0, h, :, QK_NOPE:QK_DIM] = k_rope
        v_ref[0, h] = kv[:, MLA_WIDTH + h * V_HEAD:MLA_WIDTH + (h + 1) * V_HEAD].astype(BF16)


def _od_lat_kernel(x_ref, sc_ref, sh_ref, nw_ref, win_ref, qan_ref, kvn_ref, wqb_ref, wkvb_ref,
                   gqn_ref, gqr_ref, gqrs_ref, gkn_ref, gkr_ref, gkrs_ref, cos_ref, sin_ref,
                   kin_ref, vin_ref, q_ref, k_ref, v_ref, z_ref):
    del kin_ref, vin_ref
    h = _modulated_norm(x_ref[0], nw_ref[...], sc_ref[0], sh_ref[0]).astype(BF16)
    p = jnp.dot(h, win_ref[...], preferred_element_type=F32)
    kv_off = Q_LORA
    z_off = Q_LORA + KV_LORA + 2 * QK_ROPE
    z_ref[0] = p[:, z_off:]
    cos = cos_ref[...]
    sin = sin_ref[...]
    _keys_values(p[:, kv_off:z_off], kvn_ref[...], wkvb_ref[...], gkn_ref[...], gkr_ref[...], gkrs_ref[...],
                 cos[:, :QK_ROPE], sin[:, :QK_ROPE], k_ref, v_ref)

    q = jnp.dot(_head_rms(p[:, :Q_LORA], qan_ref[...]).astype(BF16), wqb_ref[...],
                preferred_element_type=F32)
    rw = MLA_HEADS * QK_ROPE
    qr = q[:, MLA_WIDTH:MLA_WIDTH + rw]
    qr_sw = q[:, MLA_WIDTH + rw:MLA_WIDTH + 2 * rw]
    rot = (qr * gqr_ref[...]) * cos + (qr_sw * gqrs_ref[...]) * sin
    for hd in range(MLA_HEADS):
        qn = _head_rms(q[:, hd * QK_NOPE:(hd + 1) * QK_NOPE], gqn_ref[...]) * SM_SCALE
        q_ref[0, hd, :, 0:QK_NOPE] = qn.astype(BF16)
        sl = slice(hd * QK_ROPE, (hd + 1) * QK_ROPE)
        t = qr[:, sl]
        rs = lax.rsqrt(jnp.mean(t * t, axis=-1, keepdims=True) + NORM_EPS) * SM_SCALE
        q_ref[0, hd, :, QK_NOPE:QK_DIM] = (rot[:, sl] * rs).astype(BF16)


def _od_ctx_kernel(x_ref, sc_ref, sh_ref, nw_ref, win_ref, kvn_ref, wkvb_ref,
                   gkn_ref, gkr_ref, gkrs_ref, k_ref, v_ref):
    h = _modulated_norm(x_ref[0], nw_ref[...], sc_ref[0], sh_ref[0]).astype(BF16)
    p = jnp.dot(h, win_ref[...], preferred_element_type=F32)
    _keys_values(p, kvn_ref[...], wkvb_ref[...], gkn_ref[...], gkr_ref[...], gkrs_ref[...],
                 None, None, k_ref, v_ref)


def _od_ctx_call(ctx, scale, shift, nw, win_kv, kvn, wkvb, gkn, gkr, gkrs, t_all, tm):
    nb, tc, d = ctx.shape
    row = lambda b, i: (b, i, 0)
    vec = lambda b, i: (b, 0, 0)
    head = lambda b, i: (b, 0, i, 0)
    params = [nw, win_kv, kvn, wkvb, gkn, gkr, gkrs]
    return pl.pallas_call(
        _od_ctx_kernel,
        grid=(nb, tc // tm),
        in_specs=[pl.BlockSpec((1, tm, d), row), pl.BlockSpec((1, 1, d), vec), pl.BlockSpec((1, 1, d), vec)]
                 + [_full(p.shape) for p in params],
        out_specs=[pl.BlockSpec((1, MLA_HEADS, tm, QK_DIM), head),
                   pl.BlockSpec((1, MLA_HEADS, tm, V_HEAD), head)],
        out_shape=[jax.ShapeDtypeStruct((nb, MLA_HEADS, t_all, QK_DIM), BF16),
                   jax.ShapeDtypeStruct((nb, MLA_HEADS, t_all, V_HEAD), BF16)],
        compiler_params=_cparams("parallel", "parallel"),
        name="od_in_ctx",
    )(ctx, scale, shift, *params)


def _od_lat_call(x, scale, shift, params, cos, sin, k_all, v_all, ctx_len, tm):
    nb, t, d = x.shape
    t_all = k_all.shape[2]
    off = ctx_len // tm
    row = lambda b, i: (b, i, 0)
    vec = lambda b, i: (b, 0, 0)
    head_q = lambda b, i: (b, 0, i, 0)
    head_k = lambda b, i: (b, 0, i + off, 0)
    tab = lambda b, i: (i, 0)
    rw = MLA_HEADS * QK_ROPE
    n_in = 3 + len(params) + 2
    return pl.pallas_call(
        _od_lat_kernel,
        grid=(nb, t // tm),
        in_specs=[pl.BlockSpec((1, tm, d), row), pl.BlockSpec((1, 1, d), vec), pl.BlockSpec((1, 1, d), vec)]
                 + [_full(p.shape) for p in params]
                 + [pl.BlockSpec((tm, rw), tab), pl.BlockSpec((tm, rw), tab),
                    pl.BlockSpec(memory_space=pl.ANY), pl.BlockSpec(memory_space=pl.ANY)],
        out_specs=[pl.BlockSpec((1, MLA_HEADS, tm, QK_DIM), head_q),
                   pl.BlockSpec((1, MLA_HEADS, tm, QK_DIM), head_k),
                   pl.BlockSpec((1, MLA_HEADS, tm, V_HEAD), head_k),
                   pl.BlockSpec((1, tm, MLA_WIDTH), row)],
        out_shape=[jax.ShapeDtypeStruct((nb, MLA_HEADS, t, QK_DIM), BF16),
                   jax.ShapeDtypeStruct(k_all.shape, BF16),
                   jax.ShapeDtypeStruct(v_all.shape, BF16),
                   jax.ShapeDtypeStruct((nb, t, MLA_WIDTH), F32)],
        input_output_aliases={n_in: 1, n_in + 1: 2},
        compiler_params=_cparams("parallel", "parallel"),
        name="od_in_lat",
    )(x, scale, shift, *params, cos, sin, k_all, v_all)


def _attn_kernel(q_ref, k_ref, v_ref, o_ref):
    s = lax.dot_general(q_ref[0, 0], k_ref[0, 0], (((1,), (1,)), ((), ())), preferred_element_type=F32)
    m = jnp.max(s, axis=-1, keepdims=True)
    p = jnp.exp(s - m)
    l = jnp.sum(p, axis=-1, keepdims=True)
    o = jnp.dot(p.astype(BF16), v_ref[0, 0], preferred_element_type=F32)
    o_ref[0] = o / l


def _attn_call(q, k, v, tq):
    nb, nh, t, _ = q.shape
    tk = k.shape[2]
    return pl.pallas_call(
        _attn_kernel,
        grid=(nb, nh, t // tq),
        in_specs=[pl.BlockSpec((1, 1, tq, QK_DIM), lambda b, h, i: (b, h, i, 0)),
                  pl.BlockSpec((1, 1, tk, QK_DIM), lambda b, h, i: (b, h, 0, 0)),
                  pl.BlockSpec((1, 1, tk, V_HEAD), lambda b, h, i: (b, h, 0, 0))],
        out_specs=pl.BlockSpec((1, tq, V_HEAD), lambda b, h, i: (b, i, h)),
        out_shape=jax.ShapeDtypeStruct((nb, t, nh * V_HEAD), F32),
        compiler_params=_cparams("parallel", "parallel", "parallel"),
        name="mla_attn",
    )(q, k, v)


def _od_out_kernel(o_ref, z_ref, wo_ref, x_ref, g_ref, out_ref):
    a = (o_ref[0] * _silu(z_ref[0])).astype(BF16)
    out_ref[0] = x_ref[0] + g_ref[0] * jnp.dot(a, wo_ref[...], preferred_element_type=F32)


def _od_out_call(o, z, wo_bf16, x, gate, tm):
    nb, t, d = x.shape
    row = lambda b, i: (b, i, 0)
    return pl.pallas_call(
        _od_out_kernel,
        grid=(nb, t // tm),
        in_specs=[pl.BlockSpec((1, tm, MLA_WIDTH), row), pl.BlockSpec((1, tm, MLA_WIDTH), row),
                  _full(wo_bf16.shape), pl.BlockSpec((1, tm, d), row),
                  pl.BlockSpec((1, 1, d), lambda b, i: (b, 0, 0))],
        out_specs=pl.BlockSpec((1, tm, d), row),
        out_shape=jax.ShapeDtypeStruct((nb, t, d), F32),
        compiler_params=_cparams("parallel", "parallel"),
        name="od_out",
    )(o, z, wo_bf16, x, gate)


def _rope_tables(n):
    rows = n // GRID_W
    row = jnp.repeat(jnp.arange(rows, dtype=F32), GRID_W)
    col = jnp.tile(jnp.arange(GRID_W, dtype=F32), rows)
    inv_freq = ROPE_THETA ** (-jnp.arange(ROPE_PAIRS, dtype=F32) / ROPE_PAIRS)
    ar = row[:, None] * inv_freq
    ac = col[:, None] * inv_freq
    cos = jnp.concatenate([jnp.cos(ar), jnp.cos(ar), jnp.cos(ac), jnp.cos(ac)], axis=1)
    sin = jnp.concatenate([-jnp.sin(ar), jnp.sin(ar), -jnp.sin(ac), jnp.sin(ac)], axis=1)
    return cos, sin


def _row_tile(t, want):
    tm = min(t, want)
    assert t % tm == 0 and tm % HALO == 0
    return tm


def kernel(x, c, ctx, c_ctx, ada_w, ada_b, norm_w, ev_w_in, ev_conv_w, ev_mu, ev_k_k, ev_k_a, ev_w0, ev_w2, ev_a0, ev_a2, ev_r_k, ev_lnx_w, ev_lnx_b, ev_w_out, od_w_in, od_q_a_norm, od_kv_a_norm, od_w_qb, od_w_kvb, od_gq_nope, od_gq_rope, od_gk_nope, od_gk_rope, od_w_o):
    nb, t, d = x.shape
    tc = ctx.shape[1]
    assert t % CHUNK == 0 and tc % CHUNK == 0 and t % GRID_W == 0

    rows = -(-(nb + 1) // HALO) * HALO
    cc = jnp.zeros((rows, d), F32).at[:nb].set(c).at[nb].set(c_ctx)
    mod = _ada_call(cc, ada_w, ada_b)

    def mods(layer):
        m = mod[layer]
        lat = [m[:nb, i * d:(i + 1) * d].reshape(nb, 1, d) for i in range(3)]
        cx = [jnp.broadcast_to(m[nb, i * d:(i + 1) * d].reshape(1, 1, d), (nb, 1, d)) for i in range(3)]
        return lat, cx

    e_np = (np.arange(RWKV_WIDTH)[:, None] // RWKV_HEAD == np.arange(RWKV_WIDTH)[None, :] // RWKV_HEAD)
    e_bf16 = jnp.asarray(e_np, BF16)

    (sh_l, sc_l, g_l), (sh_c, sc_c, g_c) = mods(0)
    nw0 = norm_w[0].reshape(1, d)
    w_in0 = ev_w_in[0].astype(BF16)
    tm_l = _row_tile(t, 256)
    tm_c = _row_tile(tc, 256)
    pa_c, pr_c, zb_c = _ev_in_call(ctx, sc_c, sh_c, nw0, w_in0, tm_c)
    pa_l, pr_l, zb_l = _ev_in_call(x, sc_l, sh_l, nw0, w_in0, tm_l)

    W = RWKV_WIDTH
    vec = lambda a: a.reshape(1, -1)
    ys = {}
    for dr, reverse in enumerate((False, True)):
        lora = jnp.zeros((2 * LORA, 2 * W), F32)
        lora = lora.at[:LORA, :W].set(ev_w2[0, dr]).at[LORA:, W:].set(ev_a2[0, dr])
        params = [vec(ev_mu[0]), vec(ev_k_k[0]), vec(ev_k_a[0]), vec(ev_r_k[0]),
                  vec(ev_w0[0, dr]), vec(ev_a0[0, dr]), lora, e_bf16]
        y_c, b_c, s_c = _scan_call(pr_c, params, None, reverse)
        y_l, b_l, _ = _scan_call(pr_l, params, s_c, reverse)
        ys[dr] = (y_c, b_c, y_l, b_l)

    wo0 = ev_w_out[0].astype(BF16)
    lw, lb = vec(ev_lnx_w[0]), vec(ev_lnx_b[0])
    x1 = _ev_out_call(pa_l, zb_l, ys[0][2], ys[1][2], ys[0][3], ys[1][3], ev_conv_w[0], lw, lb,
                      e_bf16, wo0, x, g_l, tm_l)
    ctx1 = _ev_out_call(pa_c, zb_c, ys[0][0], ys[1][0], ys[0][1], ys[1][1], ev_conv_w[0], lw, lb,
                        e_bf16, wo0, ctx, g_c, tm_c)

    (sh_l, sc_l, g_l), (sh_c, sc_c, _) = mods(1)
    nw1 = norm_w[1].reshape(1, d)
    swap = np.arange(QK_ROPE) ^ ROPE_PAIRS
    hq = np.arange(MLA_HEADS)[:, None] * QK_DIM
    idx_q = np.concatenate([(hq + np.arange(QK_NOPE)).ravel(),
                            (hq + QK_NOPE + np.arange(QK_ROPE)).ravel(),
                            (hq + QK_NOPE + swap).ravel()])
    hk = np.arange(MLA_HEADS)[:, None] * (QK_NOPE + V_HEAD)
    idx_kv = np.concatenate([(hk + np.arange(QK_NOPE)).ravel(), (hk + QK_NOPE + np.arange(V_HEAD)).ravel()])
    r_off = Q_LORA + KV_LORA
    idx_in = np.concatenate([np.arange(r_off + QK_ROPE), r_off + swap,
                             np.arange(r_off + QK_ROPE, od_w_in.shape[2])])
    w_in1 = od_w_in[0][:, idx_in].astype(BF16)
    wqb = od_w_qb[0][:, idx_q].astype(BF16)
    wkvb = od_w_kvb[0][:, idx_kv].astype(BF16)
    gqr = jnp.tile(od_gq_rope[0], MLA_HEADS).reshape(1, -1)
    gqrs = jnp.tile(od_gq_rope[0][swap], MLA_HEADS).reshape(1, -1)
    gkr, gkrs = vec(od_gk_rope[0]), vec(od_gk_rope[0][swap])
    gqn, gkn = vec(od_gq_nope[0]), vec(od_gk_nope[0])
    qan, kvn = vec(od_q_a_norm[0]), vec(od_kv_a_norm[0])
    cos, sin = _rope_tables(t)
    cos = jnp.tile(cos, (1, MLA_HEADS))
    sin = jnp.tile(sin, (1, MLA_HEADS))

    tm_c = _row_tile(tc, 256)
    tm_l = _row_tile(t, tm_c)
    k_all, v_all = _od_ctx_call(ctx1, sc_c, sh_c, nw1, w_in1[:, Q_LORA:Q_LORA + KV_LORA + 2 * QK_ROPE],
                                kvn, wkvb, gkn, gkr, gkrs, tc + t, tm_c)
    params = [nw1, w_in1, qan, kvn, wqb, wkvb, gqn, gqr, gqrs, gkn, gkr, gkrs]
    q, k_all, v_all, z = _od_lat_call(x1, sc_l, sh_l, params, cos, sin, k_all, v_all, tc, tm_l)
    o = _attn_call(q, k_all, v_all, _row_tile(t, 256))
    return _od_out_call(o, z, od_w_o[0].astype(BF16), x1, g_l, _row_tile(t, 256))
```

```python
import functools

import numpy as np
import jax
import jax.numpy as jnp
from jax import lax
from jax.experimental import pallas as pl
from jax.experimental.pallas import tpu as pltpu

F32 = jnp.float32
BF16 = jnp.bfloat16
HIGHEST = lax.Precision.HIGHEST

NORM_EPS = 1e-6
GRID_W = 64

CONV_WIDTH = 512
RWKV_HEAD = 64
RWKV_WIDTH = 512
LORA = 64
RWKV_GN_EPS = 64e-5
A_COLS = 4 * CONV_WIDTH
SHIFT_COLS = 3 * RWKV_WIDTH + 2 * LORA
CHUNK = 64
PAIR = 2 * RWKV_HEAD
N_PAIRS = RWKV_WIDTH // PAIR

MLA_HEADS = 8
QK_NOPE = 128
QK_ROPE = 64
QK_DIM = QK_NOPE + QK_ROPE
V_HEAD = 128
Q_LORA = 384
KV_LORA = 256
MLA_WIDTH = MLA_HEADS * V_HEAD
SM_SCALE = QK_DIM ** -0.5
ROPE_THETA = 10000.0
ROPE_PAIRS = QK_ROPE // 4

VMEM_LIMIT_BYTES = 56 * 1024 * 1024
HALO = 8


def _cparams(*sem):
    return pltpu.CompilerParams(dimension_semantics=sem, vmem_limit_bytes=VMEM_LIMIT_BYTES)


def _silu(x):
    return x / (1.0 + jnp.exp(-x))


def _dot(a, b):
    return jnp.dot(a.astype(BF16), b.astype(BF16), preferred_element_type=F32)


def _dot_nt(a, b):
    return lax.dot_general(a.astype(BF16), b.astype(BF16), (((1,), (1,)), ((), ())),
                           preferred_element_type=F32)


def _dot_f32(a, b):
    return jnp.dot(a, b, precision=HIGHEST, preferred_element_type=F32)


def _bf16_split(x):
    hi = x.astype(BF16)
    return hi, (x - hi.astype(F32)).astype(BF16)


def _mm(a, b):
    return jnp.dot(a, b, preferred_element_type=F32)


def _dot3(a, b_hi, b_lo):
    a_hi, a_lo = _bf16_split(a)
    return _mm(a_hi, b_hi) + (_mm(a_lo, b_hi) + _mm(a_hi, b_lo))


def _exact_lhs_dot(m_bf16, x):
    hi = x.astype(BF16)
    r1 = x - hi.astype(F32)
    mid = r1.astype(BF16)
    lo = (r1 - mid.astype(F32)).astype(BF16)
    return _mm(m_bf16, hi) + (_mm(m_bf16, mid) + _mm(m_bf16, lo))


def _segsum(x, e2):
    out = []
    for i in range(x.shape[1] // PAIR):
        hi, lo = _bf16_split(x[:, i * PAIR:(i + 1) * PAIR])
        out.append(_mm(hi, e2) + _mm(lo, e2))
    return jnp.concatenate(out, axis=1)


def _full(shape):
    return pl.BlockSpec(shape, lambda *_: (0,) * len(shape))


def _ada_kernel(c_ref, w_ref, b_ref, o_ref):
    o_ref[0] = _dot_f32(_silu(c_ref[...]), w_ref[0]) + b_ref[0]


def _ada_call(cc, ada_w, ada_b):
    depth, d, d3 = ada_w.shape
    rows = cc.shape[0]
    tn = 1024
    return pl.pallas_call(
        _ada_kernel,
        grid=(depth, d3 // tn),
        in_specs=[pl.BlockSpec((rows, d), lambda l, j: (0, 0)),
                  pl.BlockSpec((1, d, tn), lambda l, j: (l, 0, j)),
                  pl.BlockSpec((1, 1, tn), lambda l, j: (l, 0, j))],
        out_specs=pl.BlockSpec((1, rows, tn), lambda l, j: (l, 0, j)),
        out_shape=jax.ShapeDtypeStruct((depth, rows, d3), F32),
        compiler_params=_cparams("arbitrary", "arbitrary"),
        name="ada_mod",
    )(cc, ada_w, ada_b.reshape(depth, 1, d3))


def _modulated_norm(x, nw, scale, shift):
    rs = lax.rsqrt(jnp.mean(x * x, axis=-1, keepdims=True) + NORM_EPS)
    return (x * rs * nw) * (1.0 + scale) + shift


def _ev_in_kernel(x_ref, sc_ref, sh_ref, nw_ref, w_ref, pa_ref, pr_ref, zb_ref):
    h = _modulated_norm(x_ref[0], nw_ref[...], sc_ref[0], sh_ref[0]).astype(BF16)
    pa_ref[0] = jnp.dot(h, w_ref[:, :A_COLS], preferred_element_type=F32)
    pr_ref[0] = jnp.dot(h, w_ref[:, A_COLS:A_COLS + SHIFT_COLS], preferred_element_type=F32)
    zb_ref[0] = jnp.dot(h, w_ref[:, A_COLS + SHIFT_COLS:], preferred_element_type=F32)


def _ev_in_call(x, scale, shift, nw, w_bf16, tm):
    nb, t, d = x.shape
    ncol = w_bf16.shape[1]
    row = lambda b, i: (b, i, 0)
    vec = lambda b, i: (b, 0, 0)
    return pl.pallas_call(
        _ev_in_kernel,
        grid=(nb, t // tm),
        in_specs=[pl.BlockSpec((1, tm, d), row),
                  pl.BlockSpec((1, 1, d), vec),
                  pl.BlockSpec((1, 1, d), vec),
                  _full((1, d)),
                  _full((d, ncol))],
        out_specs=[pl.BlockSpec((1, tm, A_COLS), row),
                   pl.BlockSpec((1, tm, SHIFT_COLS), row),
                   pl.BlockSpec((1, tm, RWKV_WIDTH), row)],
        out_shape=[jax.ShapeDtypeStruct((nb, t, A_COLS), F32),
                   jax.ShapeDtypeStruct((nb, t, SHIFT_COLS), F32),
                   jax.ShapeDtypeStruct((nb, t, RWKV_WIDTH), F32)],
        compiler_params=_cparams("parallel", "parallel"),
        name="ev_in",
    )(x, scale, shift, nw, w_bf16)


def _neighbours(cur, prev_halo, next_halo, is_first, is_last):
    n = cur.shape[0]
    rowi = lax.broadcasted_iota(jnp.int32, cur.shape, 0)
    prev_row = jnp.where(is_first, 0.0, prev_halo[HALO - 1:HALO, :])
    next_row = jnp.where(is_last, 0.0, next_halo[0:1, :])
    prev = jnp.where(rowi == 0, prev_row, pltpu.roll(cur, 1, 0))
    nxt = jnp.where(rowi == n - 1, next_row, pltpu.roll(cur, n - 1, 0))
    return prev, nxt


def _scan_prep(reverse, pr, prev_halo, next_halo, is_first, is_last,
               mu, k_k, k_a, r_k, w0, a0, lora_hi, lora_lo, e2, bon_ref):
    C = CHUNK
    W = RWKV_WIDTH
    prev, nxt = _neighbours(pr, prev_halo, next_halo, is_first, is_last)
    x = pr + mu * (0.5 * (prev + nxt) - pr)
    r = x[:, 0:W]
    k = x[:, W:2 * W]
    v = x[:, 2 * W:3 * W]
    wa = x[:, 3 * W:3 * W + 2 * LORA]

    lane = lax.broadcasted_iota(jnp.int32, (C, PAIR), 1)
    tw = jnp.where(lane < LORA, jnp.tanh(wa), wa)
    lo = _dot3(tw, lora_hi, lora_lo)
    nz = -(w0 + lo[:, :W])
    softplus = jnp.maximum(nz, 0.0) + jnp.log1p(jnp.exp(-jnp.abs(nz)))
    lw = -jnp.exp(-softplus - 0.5)
    iclr = 1.0 / (1.0 + jnp.exp(-(a0 + lo[:, W:])))

    kkv = k * k_k
    kkn = kkv / jnp.maximum(jnp.sqrt(_segsum(kkv * kkv, e2)), 1e-12)
    k_dir = k * (1.0 + (iclr - 1.0) * k_a)
    bvec = kkn * iclr
    bon_ref[0] = _segsum(r * k_dir * r_k, e2) * v

    ti = lax.broadcasted_iota(jnp.int32, (C, C), 0)
    si = lax.broadcasted_iota(jnp.int32, (C, C), 1)
    tri = ((si >= ti) if reverse else (si <= ti)).astype(BF16)
    g_in = _exact_lhs_dot(tri, lw)
    g_tot = jnp.sum(lw, axis=0, keepdims=True)
    gam_inv = jnp.exp(-g_in)
    gam_tail = jnp.exp(g_tot - g_in)
    return dict(rh=r * jnp.exp(g_in), ah=-kkn * jnp.exp(g_in - lw), bc=bvec * gam_inv, kc=k_dir * gam_inv,
                bt=bvec * gam_tail, kt=k_dir * gam_tail, v=v, gam_c=jnp.exp(g_tot))


def _scan_kernel(n_chunks, has_s0, *refs):
    refs = list(refs)
    src_refs = (refs[0:3], refs[3:6])
    (mu_ref, kk_ref, ka_ref, rk_ref, w0_ref, a0_ref, lh_ref, ll_ref, e_ref) = refs[6:15]
    pos = 15
    s0_ref = None
    if has_s0:
        s0_ref = refs[pos]
        pos += 1
    y_refs = refs[pos:pos + 2]
    bon_refs = refs[pos + 2:pos + 4]
    sfin_ref, st_ref = refs[pos + 4:pos + 6]

    C = CHUNK
    c = pl.program_id(1)
    cpos = (c, n_chunks - 1 - c)

    @pl.when(c == 0)
    def _():
        if has_s0:
            st_ref[...] = s0_ref[0]
        else:
            st_ref[...] = jnp.zeros_like(st_ref)

    e2 = e_ref[...]
    prep = []
    for d in range(2):
        cur, prv, nxt = src_refs[d]
        prep.append(_scan_prep(d == 1, cur[0], prv[0], nxt[0], cpos[d] == 0, cpos[d] == n_chunks - 1,
                               mu_ref[...], kk_ref[...], ka_ref[...], rk_ref[...], w0_ref[d], a0_ref[d],
                               lh_ref[d], ll_ref[d], e2, bon_refs[d]))

    n2 = 2 * C
    ri = lax.broadcasted_iota(jnp.int32, (n2, n2), 0)
    ci = lax.broadcasted_iota(jnp.int32, (n2, n2), 1)
    same_head = (ri >= C) == (ci >= C)
    rt = ri & (C - 1)
    ct = ci & (C - 1)
    strict = (same_head & (ct < rt), same_head & (ct > rt))
    incl = (same_head & (ct <= rt), same_head & (ct >= rt))
    diag = ri == ci
    eye = diag.astype(F32)
    blocks = [same_head & ((rt >> s) == (ct >> s)) for s in range(1, C.bit_length())]
    first_head = lax.broadcasted_iota(jnp.int32, (C, PAIR), 1) < RWKV_HEAD

    def stk(z):
        return jnp.concatenate([jnp.where(first_head, z, 0.0), jnp.where(first_head, 0.0, z)], axis=0)

    def dup(z):
        return jnp.concatenate([z, z], axis=0)

    chains = [(d, slice(p * PAIR, (p + 1) * PAIR), p) for d in range(2) for p in range(N_PAIRS)]
    ah_s = [stk(prep[d]["ah"][:, sl]) for d, sl, _ in chains]
    rh_s = [stk(prep[d]["rh"][:, sl]) for d, sl, _ in chains]
    v_s = [stk(prep[d]["v"][:, sl]) for d, sl, _ in chains]
    pm = [_dot_nt(jnp.concatenate([ah_s[i], rh_s[i]], axis=0),
                  jnp.concatenate([dup(prep[d]["bc"][:, sl]), dup(prep[d]["kc"][:, sl])], axis=0))
          for i, (d, sl, _) in enumerate(chains)]
    a_ab = [jnp.where(strict[d], pm[i][:n2, :n2], 0.0) for i, (d, _, _) in enumerate(chains)]
    a_ak = [jnp.where(strict[d], pm[i][:n2, n2:], 0.0) for i, (d, _, _) in enumerate(chains)]
    a_r = [jnp.concatenate([jnp.where(incl[d], pm[i][n2:, :n2], 0.0),
                            jnp.where(incl[d], pm[i][n2:, n2:], 0.0)], axis=1)
           for i, (d, _, _) in enumerate(chains)]
    akv = [_dot(a_ak[i], v_s[i]) for i in range(len(chains))]
    tinv = [jnp.where(blocks[0], a, 0.0) + eye for a in a_ab]
    for lvl in range(1, len(blocks)):
        off = blocks[lvl] & ~blocks[lvl - 1]
        t1 = [_dot(jnp.where(off, a_ab[i], 0.0), tinv[i]) for i in range(len(chains))]
        t2 = [_dot(tinv[i], t1[i]) for i in range(len(chains))]
        tinv = [tinv[i] + t2[i] for i in range(len(chains))]
    xx = [_dot(tinv[i], jnp.concatenate([ah_s[i], akv[i]], axis=1)) for i in range(len(chains))]
    zeros = jnp.zeros((n2, PAIR), F32)
    rr = []
    for i, (d, sl, _) in enumerate(chains):
        rhs = jnp.concatenate([xx[i], jnp.concatenate([zeros, v_s[i]], axis=1)], axis=0)
        lhs = jnp.concatenate(
            [a_r[i], jnp.concatenate([stk(prep[d]["bt"][:, sl]).T, stk(prep[d]["kt"][:, sl]).T], axis=1)], axis=0)
        rr.append(_dot(lhs, rhs))
    for i, (d, sl, p) in enumerate(chains):
        q_t = rh_s[i] + rr[i][:n2, :PAIR]
        m_w = jnp.where(diag, prep[d]["gam_c"][:, sl], 0.0) + rr[i][n2:, :PAIR]
        ff = _dot3(jnp.concatenate([q_t, m_w], axis=0), *_bf16_split(st_ref[d, p]))
        ys = ff[:n2] + rr[i][:n2, PAIR:]
        st_ref[d, p] = ff[n2:] + rr[i][n2:, PAIR:]
        y_refs[d][0, :, sl] = ys[:C] + ys[C:]

    @pl.when(c == n_chunks - 1)
    def _():
        sfin_ref[0] = st_ref[...]


def _scan_call(pr, params, s0):
    nb, t, _ = pr.shape
    C = CHUNK
    n_chunks = t // C
    hb = C // HALO
    n_hblk = t // HALO

    def specs(cs_of):
        return [pl.BlockSpec((1, C, SHIFT_COLS), lambda b, c: (b, cs_of(c), 0)),
                pl.BlockSpec((1, HALO, SHIFT_COLS), lambda b, c: (b, jnp.maximum(cs_of(c) * hb - 1, 0), 0)),
                pl.BlockSpec((1, HALO, SHIFT_COLS), lambda b, c: (b, jnp.minimum((cs_of(c) + 1) * hb, n_hblk - 1), 0))]

    fwd = lambda c: c
    rev = lambda c: n_chunks - 1 - c
    st_spec = pl.BlockSpec((1, 2, N_PAIRS, PAIR, PAIR), lambda b, c: (b, 0, 0, 0, 0))
    in_specs = specs(fwd) + specs(rev) + [_full(p.shape) for p in params]
    args = [pr] * 6 + list(params)
    if s0 is not None:
        in_specs.append(st_spec)
        args.append(s0)
    out_f = pl.BlockSpec((1, C, RWKV_WIDTH), lambda b, c: (b, c, 0))
    out_b = pl.BlockSpec((1, C, RWKV_WIDTH), lambda b, c: (b, n_chunks - 1 - c, 0))
    tok = jax.ShapeDtypeStruct((nb, t, RWKV_WIDTH), F32)
    return pl.pallas_call(
        functools.partial(_scan_kernel, n_chunks, s0 is not None),
        grid=(nb, n_chunks),
        in_specs=in_specs,
        out_specs=[out_f, out_b, out_f, out_b, st_spec],
        out_shape=[tok, tok, tok, tok, jax.ShapeDtypeStruct((nb, 2, N_PAIRS, PAIR, PAIR), F32)],
        scratch_shapes=[pltpu.VMEM((2, N_PAIRS, PAIR, PAIR), F32)],
        compiler_params=_cparams("parallel", "arbitrary"),
        name="rwkv_scan",
    )(*args)


def _ev_out_kernel(n_tiles, pa_ref, pap_ref, pan_ref, zb_ref, yf_ref, yb_ref, bf_ref, bb_ref,
                   cw_ref, lw_ref, lb_ref, e_ref, wo_ref, x_ref, g_ref, o_ref):
    i = pl.program_id(1)
    W = CONV_WIDTH
    pa = pa_ref[0]
    u, gate_b, gate_c, z = pa[:, 0:W], pa[:, W:2 * W], pa[:, 2 * W:3 * W], pa[:, 3 * W:4 * W]
    cu = gate_c * u
    cu_p = pap_ref[0][:, 2 * W:3 * W] * pap_ref[0][:, 0:W]
    cu_n = pan_ref[0][:, 2 * W:3 * W] * pan_ref[0][:, 0:W]
    prev, nxt = _neighbours(cu, cu_p, cu_n, i == 0, i == n_tiles - 1)
    cw = cw_ref[...]
    conv = cw[0:1] * prev + cw[1:2] * cu + cw[2:3] * nxt
    a_out = gate_b * conv * _silu(z)

    e = e_ref[...]
    ysum = yf_ref[0] + yb_ref[0]
    inv_n = 1.0 / RWKV_HEAD
    mean = _segsum(ysum, e) * inv_n
    dlt = ysum - mean
    var = _segsum(dlt * dlt, e) * inv_n
    rw = dlt * lax.rsqrt(var + RWKV_GN_EPS) * lw_ref[...] + lb_ref[...] + (bf_ref[0] + bb_ref[0])
    b_out = rw * _silu(zb_ref[0])

    y = (jnp.dot(a_out.astype(BF16), wo_ref[:W, :], preferred_element_type=F32)
         + jnp.dot(b_out.astype(BF16), wo_ref[W:, :], preferred_element_type=F32))
    o_ref[0] = x_ref[0] + g_ref[0] * y


def _ev_out_call(pa, zb, yf, yb, bf, bb, conv_w, lnx_w, lnx_b, e_bf16, wo_bf16, x, gate, tm):
    nb, t, d = x.shape
    n_tiles = t // tm
    hb = tm // HALO
    n_hblk = t // HALO
    row = lambda b, i: (b, i, 0)
    prv = lambda b, i: (b, jnp.maximum(i * hb - 1, 0), 0)
    nxt = lambda b, i: (b, jnp.minimum((i + 1) * hb, n_hblk - 1), 0)
    vec = lambda b, i: (b, 0, 0)
    wspec = pl.BlockSpec((1, tm, RWKV_WIDTH), row)
    return pl.pallas_call(
        functools.partial(_ev_out_kernel, n_tiles),
        grid=(nb, n_tiles),
        in_specs=[pl.BlockSpec((1, tm, A_COLS), row),
                  pl.BlockSpec((1, HALO, A_COLS), prv),
                  pl.BlockSpec((1, HALO, A_COLS), nxt),
                  wspec, wspec, wspec, wspec, wspec,
                  _full(conv_w.shape), _full(lnx_w.shape), _full(lnx_b.shape),
                  _full(e_bf16.shape), _full(wo_bf16.shape),
                  pl.BlockSpec((1, tm, d), row),
                  pl.BlockSpec((1, 1, d), vec)],
        out_specs=pl.BlockSpec((1, tm, d), row),
        out_shape=jax.ShapeDtypeStruct((nb, t, d), F32),
        compiler_params=_cparams("parallel", "parallel"),
        name="ev_out",
    )(pa, pa, pa, zb, yf, yb, bf, bb, conv_w, lnx_w, lnx_b, e_bf16, wo_bf16, x, gate)


def _head_rms(t, g):
    return t * lax.rsqrt(jnp.mean(t * t, axis=-1, keepdims=True) + NORM_EPS) * g


def _keys_values(p_kv, kvn, wkvb, gkn, gkr, gkr_sw, cos, sin, k_ref, v_ref):
    kv_a = p_kv[:, :KV_LORA]
    kr = p_kv[:, KV_LORA:KV_LORA + QK_ROPE]
    kr_sw = p_kv[:, KV_LORA + QK_ROPE:KV_LORA + 2 * QK_ROPE]
    rs = lax.rsqrt(jnp.mean(kr * kr, axis=-1, keepdims=True) + NORM_EPS)
    if cos is None:
        k_rope = kr * rs * gkr
    else:
        k_rope = ((kr * gkr) * cos + (kr_sw * gkr_sw) * sin) * rs
    k_rope = k_rope.astype(BF16)
    kv = jnp.dot(_head_rms(kv_a, kvn).astype(BF16), wkvb, preferred_element_type=F32)
    for h in range(MLA_HEADS):
        kn = _head_rms(kv[:, h * QK_NOPE:(h + 1) * QK_NOPE], gkn)
        k_ref[0, h, :, 0:QK_NOPE] = kn.astype(BF16)
        k_ref[0, h, :, QK_NOPE:QK_DIM] = k_rope
        v_ref[0, h] = kv[:, MLA_WIDTH + h * V_HEAD:MLA_WIDTH + (h + 1) * V_HEAD].astype(BF16)


def _od_lat_kernel(x_ref, sc_ref, sh_ref, nw_ref, win_ref, qan_ref, kvn_ref, wqb_ref, wkvb_ref,
                   gqn_ref, gqr_ref, gqrs_ref, gkn_ref, gkr_ref, gkrs_ref, cos_ref, sin_ref,
                   kin_ref, vin_ref, q_ref, k_ref, v_ref, z_ref):
    del kin_ref, vin_ref
    h = _modulated_norm(x_ref[0], nw_ref[...], sc_ref[0], sh_ref[0]).astype(BF16)
    p = jnp.dot(h, win_ref[...], preferred_element_type=F32)
    kv_off = Q_LORA
    z_off = Q_LORA + KV_LORA + 2 * QK_ROPE
    z_ref[0] = p[:, z_off:]
    cos = cos_ref[...]
    sin = sin_ref[...]
    _keys_values(p[:, kv_off:z_off], kvn_ref[...], wkvb_ref[...], gkn_ref[...], gkr_ref[...], gkrs_ref[...],
                 cos[:, :QK_ROPE], sin[:, :QK_ROPE], k_ref, v_ref)

    q = jnp.dot(_head_rms(p[:, :Q_LORA], qan_ref[...]).astype(BF16), wqb_ref[...],
                preferred_element_type=F32)
    rw = MLA_HEADS * QK_ROPE
    qr = q[:, MLA_WIDTH:MLA_WIDTH + rw]
    qr_sw = q[:, MLA_WIDTH + rw:MLA_WIDTH + 2 * rw]
    rot = (qr * gqr_ref[...]) * cos + (qr_sw * gqrs_ref[...]) * sin
    for hd in range(MLA_HEADS):
        qn = _head_rms(q[:, hd * QK_NOPE:(hd + 1) * QK_NOPE], gqn_ref[...]) * SM_SCALE
        q_ref[0, hd, :, 0:QK_NOPE] = qn.astype(BF16)
        sl = slice(hd * QK_ROPE, (hd + 1) * QK_ROPE)
        t = qr[:, sl]
        rs = lax.rsqrt(jnp.mean(t * t, axis=-1, keepdims=True) + NORM_EPS) * SM_SCALE
        q_ref[0, hd, :, QK_NOPE:QK_DIM] = (rot[:, sl] * rs).astype(BF16)


def _od_ctx_kernel(x_ref, sc_ref, sh_ref, nw_ref, win_ref, kvn_ref, wkvb_ref,
                   gkn_ref, gkr_ref, gkrs_ref, k_ref, v_ref):
    h = _modulated_norm(x_ref[0], nw_ref[...], sc_ref[0], sh_ref[0]).astype(BF16)
    p = jnp.dot(h, win_ref[...], preferred_element_type=F32)
    _keys_values(p, kvn_ref[...], wkvb_ref[...], gkn_ref[...], gkr_ref[...], gkrs_ref[...],
                 None, None, k_ref, v_ref)


def _od_ctx_call(ctx, scale, shift, nw, win_kv, kvn, wkvb, gkn, gkr, gkrs, t_all, tm):
    nb, tc, d = ctx.shape
    row = lambda b, i: (b, i, 0)
    vec = lambda b, i: (b, 0, 0)
    head = lambda b, i: (b, 0, i, 0)
    params = [nw, win_kv, kvn, wkvb, gkn, gkr, gkrs]
    return pl.pallas_call(
        _od_ctx_kernel,
        grid=(nb, tc // tm),
        in_specs=[pl.BlockSpec((1, tm, d), row), pl.BlockSpec((1, 1, d), vec), pl.BlockSpec((1, 1, d), vec)]
                 + [_full(p.shape) for p in params],
        out_specs=[pl.BlockSpec((1, MLA_HEADS, tm, QK_DIM), head),
                   pl.BlockSpec((1, MLA_HEADS, tm, V_HEAD), head)],
        out_shape=[jax.ShapeDtypeStruct((nb, MLA_HEADS, t_all, QK_DIM), BF16),
                   jax.ShapeDtypeStruct((nb, MLA_HEADS, t_all, V_HEAD), BF16)],
        compiler_params=_cparams("parallel", "parallel"),
        name="od_in_ctx",
    )(ctx, scale, shift, *params)


def _od_lat_call(x, scale, shift, params, cos, sin, k_all, v_all, ctx_len, tm):
    nb, t, d = x.shape
    off = ctx_len // tm
    row = lambda b, i: (b, i, 0)
    vec = lambda b, i: (b, 0, 0)
    head_q = lambda b, i: (b, 0, i, 0)
    head_k = lambda b, i: (b, 0, i + off, 0)
    tab = lambda b, i: (i, 0)
    rw = MLA_HEADS * QK_ROPE
    n_in = 3 + len(params) + 2
    return pl.pallas_call(
        _od_lat_kernel,
        grid=(nb, t // tm),
        in_specs=[pl.BlockSpec((1, tm, d), row), pl.BlockSpec((1, 1, d), vec), pl.BlockSpec((1, 1, d), vec)]
                 + [_full(p.shape) for p in params]
                 + [pl.BlockSpec((tm, rw), tab), pl.BlockSpec((tm, rw), tab),
                    pl.BlockSpec(memory_space=pl.ANY), pl.BlockSpec(memory_space=pl.ANY)],
        out_specs=[pl.BlockSpec((1, MLA_HEADS, tm, QK_DIM), head_q),
                   pl.BlockSpec((1, MLA_HEADS, tm, QK_DIM), head_k),
                   pl.BlockSpec((1, MLA_HEADS, tm, V_HEAD), head_k),
                   pl.BlockSpec((1, tm, MLA_WIDTH), row)],
        out_shape=[jax.ShapeDtypeStruct((nb, MLA_HEADS, t, QK_DIM), BF16),
                   jax.ShapeDtypeStruct(k_all.shape, BF16),
                   jax.ShapeDtypeStruct(v_all.shape, BF16),
                   jax.ShapeDtypeStruct((nb, t, MLA_WIDTH), F32)],
        input_output_aliases={n_in: 1, n_in + 1: 2},
        compiler_params=_cparams("parallel", "parallel"),
        name="od_in_lat",
    )(x, scale, shift, *params, cos, sin, k_all, v_all)


def _attn_kernel(q_ref, k_ref, v_ref, o_ref):
    s = lax.dot_general(q_ref[0, 0], k_ref[0, 0], (((1,), (1,)), ((), ())), preferred_element_type=F32)
    m = jnp.max(s, axis=-1, keepdims=True)
    p = jnp.exp(s - m)
    l = jnp.sum(p, axis=-1, keepdims=True)
    o = jnp.dot(p.astype(BF16), v_ref[0, 0], preferred_element_type=F32)
    o_ref[0] = o / l


def _attn_call(q, k, v, tq):
    nb, nh, t, _ = q.shape
    tk = k.shape[2]
    return pl.pallas_call(
        _attn_kernel,
        grid=(nb, nh, t // tq),
        in_specs=[pl.BlockSpec((1, 1, tq, QK_DIM), lambda b, h, i: (b, h, i, 0)),
                  pl.BlockSpec((1, 1, tk, QK_DIM), lambda b, h, i: (b, h, 0, 0)),
                  pl.BlockSpec((1, 1, tk, V_HEAD), lambda b, h, i: (b, h, 0, 0))],
        out_specs=pl.BlockSpec((1, tq, V_HEAD), lambda b, h, i: (b, i, h)),
        out_shape=jax.ShapeDtypeStruct((nb, t, nh * V_HEAD), F32),
        compiler_params=_cparams("parallel", "parallel", "parallel"),
        name="mla_attn",
    )(q, k, v)


def _od_out_kernel(o_ref, z_ref, wo_ref, x_ref, g_ref, out_ref):
    a = (o_ref[0] * _silu(z_ref[0])).astype(BF16)
    out_ref[0] = x_ref[0] + g_ref[0] * jnp.dot(a, wo_ref[...], preferred_element_type=F32)


def _od_out_call(o, z, wo_bf16, x, gate, tm):
    nb, t, d = x.shape
    row = lambda b, i: (b, i, 0)
    return pl.pallas_call(
        _od_out_kernel,
        grid=(nb, t // tm),
        in_specs=[pl.BlockSpec((1, tm, MLA_WIDTH), row), pl.BlockSpec((1, tm, MLA_WIDTH), row),
                  _full(wo_bf16.shape), pl.BlockSpec((1, tm, d), row),
                  pl.BlockSpec((1, 1, d), lambda b, i: (b, 0, 0))],
        out_specs=pl.BlockSpec((1, tm, d), row),
        out_shape=jax.ShapeDtypeStruct((nb, t, d), F32),
        compiler_params=_cparams("parallel", "parallel"),
        name="od_out",
    )(o, z, wo_bf16, x, gate)


def _rope_tables(n):
    rows = n // GRID_W
    row = jnp.repeat(jnp.arange(rows, dtype=F32), GRID_W)
    col = jnp.tile(jnp.arange(GRID_W, dtype=F32), rows)
    inv_freq = ROPE_THETA ** (-jnp.arange(ROPE_PAIRS, dtype=F32) / ROPE_PAIRS)
    ar = row[:, None] * inv_freq
    ac = col[:, None] * inv_freq
    cos = jnp.concatenate([jnp.cos(ar), jnp.cos(ar), jnp.cos(ac), jnp.cos(ac)], axis=1)
    sin = jnp.concatenate([-jnp.sin(ar), jnp.sin(ar), -jnp.sin(ac), jnp.sin(ac)], axis=1)
    return cos, sin


def _row_tile(t, want):
    tm = min(t, want)
    assert t % tm == 0 and tm % HALO == 0
    return tm


def kernel(x, c, ctx, c_ctx, ada_w, ada_b, norm_w, ev_w_in, ev_conv_w, ev_mu, ev_k_k, ev_k_a, ev_w0, ev_w2, ev_a0, ev_a2, ev_r_k, ev_lnx_w, ev_lnx_b, ev_w_out, od_w_in, od_q_a_norm, od_kv_a_norm, od_w_qb, od_w_kvb, od_gq_nope, od_gq_rope, od_gk_nope, od_gk_rope, od_w_o):
    nb, t, d = x.shape
    tc = ctx.shape[1]
    assert t % CHUNK == 0 and tc % CHUNK == 0 and t % GRID_W == 0

    rows = -(-(nb + 1) // HALO) * HALO
    cc = jnp.zeros((rows, d), F32).at[:nb].set(c).at[nb].set(c_ctx)
    mod = _ada_call(cc, ada_w, ada_b)

    def mods(layer):
        m = mod[layer]
        lat = [m[:nb, i * d:(i + 1) * d].reshape(nb, 1, d) for i in range(3)]
        cx = [jnp.broadcast_to(m[nb, i * d:(i + 1) * d].reshape(1, 1, d), (nb, 1, d)) for i in range(3)]
        return lat, cx

    e_np = np.arange(PAIR)[:, None] // RWKV_HEAD == np.arange(PAIR)[None, :] // RWKV_HEAD
    e_bf16 = jnp.asarray(e_np, BF16)

    (sh_l, sc_l, g_l), (sh_c, sc_c, g_c) = mods(0)
    nw0 = norm_w[0].reshape(1, d)
    w_in0 = ev_w_in[0].astype(BF16)
    tm_l = _row_tile(t, 256)
    tm_c = _row_tile(tc, 256)
    pa_c, pr_c, zb_c = _ev_in_call(ctx, sc_c, sh_c, nw0, w_in0, tm_c)
    pa_l, pr_l, zb_l = _ev_in_call(x, sc_l, sh_l, nw0, w_in0, tm_l)

    W = RWKV_WIDTH
    vec = lambda a: a.reshape(1, -1)
    lora = jnp.zeros((2, 2 * LORA, 2 * W), F32)
    lora = lora.at[:, :LORA, :W].set(ev_w2[0]).at[:, LORA:, W:].set(ev_a2[0])
    lora_hi = lora.astype(BF16)
    lora_lo = (lora - lora_hi.astype(F32)).astype(BF16)
    params = [vec(ev_mu[0]), vec(ev_k_k[0]), vec(ev_k_a[0]), vec(ev_r_k[0]),
              ev_w0[0].reshape(2, 1, W), ev_a0[0].reshape(2, 1, W), lora_hi, lora_lo, e_bf16]
    yf_c, yb_c, bf_c, bb_c, s_c = _scan_call(pr_c, params, None)
    yf_l, yb_l, bf_l, bb_l, _ = _scan_call(pr_l, params, s_c)

    wo0 = ev_w_out[0].astype(BF16)
    lw, lb = vec(ev_lnx_w[0]), vec(ev_lnx_b[0])
    x1 = _ev_out_call(pa_l, zb_l, yf_l, yb_l, bf_l, bb_l, ev_conv_w[0], lw, lb, e_bf16, wo0, x, g_l, tm_l)
    ctx1 = _ev_out_call(pa_c, zb_c, yf_c, yb_c, bf_c, bb_c, ev_conv_w[0], lw, lb, e_bf16, wo0, ctx, g_c, tm_c)

    (sh_l, sc_l, g_l), (sh_c, sc_c, _) = mods(1)
    nw1 = norm_w[1].reshape(1, d)
    swap = np.arange(QK_ROPE) ^ ROPE_PAIRS
    hq = np.arange(MLA_HEADS)[:, None] * QK_DIM
    idx_q = np.concatenate([(hq + np.arange(QK_NOPE)).ravel(),
                            (hq + QK_NOPE + np.arange(QK_ROPE)).ravel(),
                            (hq + QK_NOPE + swap).ravel()])
    hk = np.arange(MLA_HEADS)[:, None] * (QK_NOPE + V_HEAD)
    idx_kv = np.concatenate([(hk + np.arange(QK_NOPE)).ravel(), (hk + QK_NOPE + np.arange(V_HEAD)).ravel()])
    r_off = Q_LORA + KV_LORA
    idx_in = np.concatenate([np.arange(r_off + QK_ROPE), r_off + swap,
                             np.arange(r_off + QK_ROPE, od_w_in.shape[2])])
    w_in1 = od_w_in[0][:, idx_in].astype(BF16)
    wqb = od_w_qb[0][:, idx_q].astype(BF16)
    wkvb = od_w_kvb[0][:, idx_kv].astype(BF16)
    gqr = jnp.tile(od_gq_rope[0], MLA_HEADS).reshape(1, -1)
    gqrs = jnp.tile(od_gq_rope[0][swap], MLA_HEADS).reshape(1, -1)
    gkr, gkrs = vec(od_gk_rope[0]), vec(od_gk_rope[0][swap])
    gqn, gkn = vec(od_gq_nope[0]), vec(od_gk_nope[0])
    qan, kvn = vec(od_q_a_norm[0]), vec(od_kv_a_norm[0])
    cos, sin = _rope_tables(t)
    cos = jnp.tile(cos, (1, MLA_HEADS))
    sin = jnp.tile(sin, (1, MLA_HEADS))

    tm_c = _row_tile(tc, 256)
    tm_l = _row_tile(t, tm_c)
    k_all, v_all = _od_ctx_call(ctx1, sc_c, sh_c, nw1, w_in1[:, Q_LORA:Q_LORA + KV_LORA + 2 * QK_ROPE],
                                kvn, wkvb, gkn, gkr, gkrs, tc + t, tm_c)
    params = [nw1, w_in1, qan, kvn, wqb, wkvb, gqn, gqr, gqrs, gkn, gkr, gkrs]
    q, k_all, v_all, z = _od_lat_call(x1, sc_l, sh_l, params, cos, sin, k_all, v_all, tc, tm_l)
    o = _attn_call(q, k_all, v_all, _row_tile(t, 256))
    return _od_out_call(o, z, od_w_o[0].astype(BF16), x1, g_l, _row_tile(t, 256))
```

```python
import functools

import numpy as np
import jax
import jax.numpy as jnp
from jax import lax
from jax.experimental import pallas as pl
from jax.experimental.pallas import tpu as pltpu

F32 = jnp.float32
BF16 = jnp.bfloat16

NORM_EPS = 1e-6
GRID_W = 64

CONV_WIDTH = 512
RWKV_HEAD = 64
RWKV_WIDTH = 512
LORA = 64
RWKV_GN_EPS = 64e-5
DECAY_SCALE = -float(np.exp(-0.5))
A_COLS = 4 * CONV_WIDTH
SHIFT_COLS = 3 * RWKV_WIDTH + 2 * LORA
CHUNK = 64
PAIR = 2 * RWKV_HEAD
N_PAIRS = RWKV_WIDTH // PAIR

MLA_HEADS = 8
QK_NOPE = 128
QK_ROPE = 64
QK_DIM = QK_NOPE + QK_ROPE
V_HEAD = 128
Q_LORA = 384
KV_LORA = 256
MLA_WIDTH = MLA_HEADS * V_HEAD
SM_SCALE = QK_DIM ** -0.5
LOG2E = 1.4426950408889634
Q_SCALE = SM_SCALE * LOG2E
ROPE_THETA = 10000.0
ROPE_PAIRS = QK_ROPE // 4

VMEM_LIMIT_BYTES = 56 * 1024 * 1024
HALO = 8


def _cparams(*sem):
    return pltpu.CompilerParams(dimension_semantics=sem, vmem_limit_bytes=VMEM_LIMIT_BYTES)


def _silu(x):
    return x / (1.0 + jnp.exp(-x))


def _dot(a, b):
    return jnp.dot(a.astype(BF16), b.astype(BF16), preferred_element_type=F32)


def _dot_nt(a, b):
    return lax.dot_general(a.astype(BF16), b.astype(BF16), (((1,), (1,)), ((), ())),
                           preferred_element_type=F32)


def _bf16_split(x):
    hi = x.astype(BF16)
    return hi, (x - hi.astype(F32)).astype(BF16)


def _mm(a, b):
    return jnp.dot(a, b, preferred_element_type=F32)


def _dot3(a, b_hi, b_lo):
    a_hi, a_lo = _bf16_split(a)
    return _mm(a_hi, b_hi) + (_mm(a_lo, b_hi) + _mm(a_hi, b_lo))


def _exact_lhs_dot(m_bf16, x):
    hi = x.astype(BF16)
    r1 = x - hi.astype(F32)
    mid = r1.astype(BF16)
    lo = (r1 - mid.astype(F32)).astype(BF16)
    return _mm(m_bf16, hi) + (_mm(m_bf16, mid) + _mm(m_bf16, lo))


def _segsum(x, e2):
    out = []
    for i in range(x.shape[1] // PAIR):
        hi, lo = _bf16_split(x[:, i * PAIR:(i + 1) * PAIR])
        out.append(_mm(hi, e2) + _mm(lo, e2))
    return jnp.concatenate(out, axis=1)


def _full(shape):
    return pl.BlockSpec(shape, lambda *_: (0,) * len(shape))


def _ada_kernel(c_ref, w_ref, b_ref, o_ref):
    o_ref[0] = _dot3(_silu(c_ref[...]), *_bf16_split(w_ref[0])) + b_ref[0]


def _ada_call(cc, ada_w, ada_b):
    depth, d, d3 = ada_w.shape
    rows = cc.shape[0]
    tn = 1024
    return pl.pallas_call(
        _ada_kernel,
        grid=(depth, d3 // tn),
        in_specs=[pl.BlockSpec((rows, d), lambda l, j: (0, 0)),
                  pl.BlockSpec((1, d, tn), lambda l, j: (l, 0, j)),
                  pl.BlockSpec((1, 1, tn), lambda l, j: (l, 0, j))],
        out_specs=pl.BlockSpec((1, rows, tn), lambda l, j: (l, 0, j)),
        out_shape=jax.ShapeDtypeStruct((depth, rows, d3), F32),
        compiler_params=_cparams("arbitrary", "arbitrary"),
        name="ada_mod",
    )(cc, ada_w, ada_b.reshape(depth, 1, d3))


def _modulated_norm(x, nw, scale, shift):
    rs = lax.rsqrt(jnp.mean(x * x, axis=-1, keepdims=True) + NORM_EPS)
    return (x * rs * nw) * (1.0 + scale) + shift


def _ev_in_kernel(x_ref, sc_ref, sh_ref, nw_ref, w_ref, pa_ref, pr_ref, zb_ref):
    h = _modulated_norm(x_ref[0], nw_ref[...], sc_ref[0], sh_ref[0]).astype(BF16)
    pa_ref[0] = jnp.dot(h, w_ref[:, :A_COLS], preferred_element_type=F32)
    pr_ref[0] = jnp.dot(h, w_ref[:, A_COLS:A_COLS + SHIFT_COLS], preferred_element_type=F32)
    zb_ref[0] = jnp.dot(h, w_ref[:, A_COLS + SHIFT_COLS:], preferred_element_type=F32)


def _ev_in_call(x, scale, shift, nw, w_bf16, tm):
    nb, t, d = x.shape
    ncol = w_bf16.shape[1]
    row = lambda b, i: (b, i, 0)
    vec = lambda b, i: (b, 0, 0)
    return pl.pallas_call(
        _ev_in_kernel,
        grid=(nb, t // tm),
        in_specs=[pl.BlockSpec((1, tm, d), row),
                  pl.BlockSpec((1, 1, d), vec),
                  pl.BlockSpec((1, 1, d), vec),
                  _full((1, d)),
                  _full((d, ncol))],
        out_specs=[pl.BlockSpec((1, tm, A_COLS), row),
                   pl.BlockSpec((1, tm, SHIFT_COLS), row),
                   pl.BlockSpec((1, tm, RWKV_WIDTH), row)],
        out_shape=[jax.ShapeDtypeStruct((nb, t, A_COLS), F32),
                   jax.ShapeDtypeStruct((nb, t, SHIFT_COLS), F32),
                   jax.ShapeDtypeStruct((nb, t, RWKV_WIDTH), F32)],
        compiler_params=_cparams("parallel", "parallel"),
        name="ev_in",
    )(x, scale, shift, nw, w_bf16)


def _neighbours(cur, prev_halo, next_halo, is_first, is_last):
    n = cur.shape[0]
    rowi = lax.broadcasted_iota(jnp.int32, cur.shape, 0)
    prev_row = jnp.where(is_first, 0.0, prev_halo[HALO - 1:HALO, :])
    next_row = jnp.where(is_last, 0.0, next_halo[0:1, :])
    prev = jnp.where(rowi == 0, prev_row, pltpu.roll(cur, 1, 0))
    nxt = jnp.where(rowi == n - 1, next_row, pltpu.roll(cur, n - 1, 0))
    return prev, nxt


def _scan_prep(reverse, pr, prev_halo, next_halo, is_first, is_last,
               mu_keep, mu_half, k_k, k_a, ka_keep, r_k, wa0, lora_hi, lora_lo, e2, bon_ref):
    C = CHUNK
    W = RWKV_WIDTH
    prev, nxt = _neighbours(pr, prev_halo, next_halo, is_first, is_last)
    x = mu_keep * pr + mu_half * (prev + nxt)
    r = x[:, 0:W]
    k = x[:, W:2 * W]
    v = x[:, 2 * W:3 * W]
    wa = x[:, 3 * W:3 * W + 2 * LORA]

    lane = lax.broadcasted_iota(jnp.int32, (C, PAIR), 1)
    tw = jnp.where(lane < LORA, jnp.tanh(wa), wa)
    lo = _dot3(tw, lora_hi, lora_lo)
    sig = 1.0 / (1.0 + jnp.exp(-(wa0 + lo)))
    lw = DECAY_SCALE * sig[:, :W]
    iclr = sig[:, W:]

    kkv = k * k_k
    kkn = kkv * lax.rsqrt(jnp.maximum(_segsum(kkv * kkv, e2), 1e-24))
    k_dir = k * (ka_keep + iclr * k_a)
    bvec = kkn * iclr
    bon_ref[0] = _segsum(r * k_dir * r_k, e2) * v

    ti = lax.broadcasted_iota(jnp.int32, (C, C), 0)
    si = lax.broadcasted_iota(jnp.int32, (C, C), 1)
    tri = ((si >= ti) if reverse else (si <= ti)).astype(BF16)
    g_in = _exact_lhs_dot(tri, lw)
    g_tot = jnp.sum(lw, axis=0, keepdims=True)
    gam_inv = jnp.exp(-g_in)
    gam_c = jnp.exp(g_tot)
    bc = bvec * gam_inv
    kc = k_dir * gam_inv
    return dict(rh=r * jnp.exp(g_in), ah=-kkn * jnp.exp(g_in - lw), bc=bc, kc=kc,
                bt=bc * gam_c, kt=kc * gam_c, v=v, gam_c=gam_c)


def _scan_kernel(n_chunks, has_s0, *refs):
    refs = list(refs)
    src_refs = (refs[0:3], refs[3:6])
    (muk_ref, muh_ref, kk_ref, ka_ref, kak_ref, rk_ref, wa0_ref, lh_ref, ll_ref, e_ref) = refs[6:16]
    pos = 16
    s0_ref = None
    if has_s0:
        s0_ref = refs[pos]
        pos += 1
    y_refs = refs[pos:pos + 2]
    bon_refs = refs[pos + 2:pos + 4]
    sfin_ref, st_ref = refs[pos + 4:pos + 6]

    C = CHUNK
    c = pl.program_id(1)
    cpos = (c, n_chunks - 1 - c)

    @pl.when(c == 0)
    def _():
        if has_s0:
            st_ref[...] = s0_ref[0]
        else:
            st_ref[...] = jnp.zeros_like(st_ref)

    e2 = e_ref[...]
    prep = []
    for d in range(2):
        cur, prv, nxt = src_refs[d]
        prep.append(_scan_prep(d == 1, cur[0], prv[0], nxt[0], cpos[d] == 0, cpos[d] == n_chunks - 1,
                               muk_ref[...], muh_ref[...], kk_ref[...], ka_ref[...], kak_ref[...], rk_ref[...],
                               wa0_ref[d], lh_ref[d], ll_ref[d], e2, bon_refs[d]))

    n2 = 2 * C
    ri = lax.broadcasted_iota(jnp.int32, (n2, n2), 0)
    ci = lax.broadcasted_iota(jnp.int32, (n2, n2), 1)
    same_head = (ri >= C) == (ci >= C)
    rt = ri & (C - 1)
    ct = ci & (C - 1)
    strict = (same_head & (ct < rt), same_head & (ct > rt))
    incl = (same_head & (ct <= rt), same_head & (ct >= rt))
    diag = ri == ci
    eye = diag.astype(F32)
    blocks = [same_head & ((rt >> s) == (ct >> s)) for s in range(1, C.bit_length())]
    first_head = lax.broadcasted_iota(jnp.int32, (C, PAIR), 1) < RWKV_HEAD

    def stk(z):
        return jnp.concatenate([jnp.where(first_head, z, 0.0), jnp.where(first_head, 0.0, z)], axis=0)

    def dup(z):
        return jnp.concatenate([z, z], axis=0)

    chains = [(d, slice(p * PAIR, (p + 1) * PAIR), p) for d in range(2) for p in range(N_PAIRS)]
    ah_s = [stk(prep[d]["ah"][:, sl]) for d, sl, _ in chains]
    rh_s = [stk(prep[d]["rh"][:, sl]) for d, sl, _ in chains]
    v_s = [stk(prep[d]["v"][:, sl]) for d, sl, _ in chains]
    pm = [_dot_nt(jnp.concatenate([ah_s[i], rh_s[i]], axis=0),
                  jnp.concatenate([dup(prep[d]["bc"][:, sl]), dup(prep[d]["kc"][:, sl])], axis=0))
          for i, (d, sl, _) in enumerate(chains)]
    a_ab = [jnp.where(strict[d], pm[i][:n2, :n2], 0.0) for i, (d, _, _) in enumerate(chains)]
    a_ak = [jnp.where(strict[d], pm[i][:n2, n2:], 0.0) for i, (d, _, _) in enumerate(chains)]
    a_r = [jnp.concatenate([jnp.where(incl[d], pm[i][n2:, :n2], 0.0),
                            jnp.where(incl[d], pm[i][n2:, n2:], 0.0)], axis=1)
           for i, (d, _, _) in enumerate(chains)]
    akv = [_dot(a_ak[i], v_s[i]) for i in range(len(chains))]
    tinv = [jnp.where(blocks[0], a, 0.0) + eye for a in a_ab]
    for lvl in range(1, len(blocks)):
        off = blocks[lvl] & ~blocks[lvl - 1]
        t1 = [_dot(jnp.where(off, a_ab[i], 0.0), tinv[i]) for i in range(len(chains))]
        t2 = [_dot(tinv[i], t1[i]) for i in range(len(chains))]
        tinv = [tinv[i] + t2[i] for i in range(len(chains))]
    xx = [_dot(tinv[i], jnp.concatenate([ah_s[i], akv[i]], axis=1)) for i in range(len(chains))]
    zeros = jnp.zeros((n2, PAIR), F32)
    rr = []
    for i, (d, sl, _) in enumerate(chains):
        rhs = jnp.concatenate([xx[i], jnp.concatenate([zeros, v_s[i]], axis=1)], axis=0)
        lhs = jnp.concatenate(
            [a_r[i], jnp.concatenate([stk(prep[d]["bt"][:, sl]).T, stk(prep[d]["kt"][:, sl]).T], axis=1)], axis=0)
        rr.append(_dot(lhs, rhs))
    for i, (d, sl, p) in enumerate(chains):
        q_t = rh_s[i] + rr[i][:n2, :PAIR]
        st = st_ref[d, p]
        st_hi, st_lo = _bf16_split(st)
        m_off = rr[i][n2:, :PAIR].astype(BF16)
        ys = _mm(q_t.astype(BF16), st_hi) + rr[i][:n2, PAIR:]
        g_col = jnp.sum(jnp.where(diag, prep[d]["gam_c"][:, sl], 0.0), axis=1, keepdims=True)
        st_ref[d, p] = (g_col * st + (_mm(m_off, st_hi) + _mm(m_off, st_lo))) + rr[i][n2:, PAIR:]
        y_refs[d][0, :, sl] = ys[:C] + ys[C:]

    @pl.when(c == n_chunks - 1)
    def _():
        sfin_ref[0] = st_ref[...]


def _scan_call(pr, params, s0):
    nb, t, _ = pr.shape
    C = CHUNK
    n_chunks = t // C
    hb = C // HALO
    n_hblk = t // HALO

    def specs(cs_of):
        return [pl.BlockSpec((1, C, SHIFT_COLS), lambda b, c: (b, cs_of(c), 0)),
                pl.BlockSpec((1, HALO, SHIFT_COLS), lambda b, c: (b, jnp.maximum(cs_of(c) * hb - 1, 0), 0)),
                pl.BlockSpec((1, HALO, SHIFT_COLS), lambda b, c: (b, jnp.minimum((cs_of(c) + 1) * hb, n_hblk - 1), 0))]

    fwd = lambda c: c
    rev = lambda c: n_chunks - 1 - c
    st_spec = pl.BlockSpec((1, 2, N_PAIRS, PAIR, PAIR), lambda b, c: (b, 0, 0, 0, 0))
    in_specs = specs(fwd) + specs(rev) + [_full(p.shape) for p in params]
    args = [pr] * 6 + list(params)
    if s0 is not None:
        in_specs.append(st_spec)
        args.append(s0)
    out_f = pl.BlockSpec((1, C, RWKV_WIDTH), lambda b, c: (b, c, 0))
    out_b = pl.BlockSpec((1, C, RWKV_WIDTH), lambda b, c: (b, n_chunks - 1 - c, 0))
    tok = jax.ShapeDtypeStruct((nb, t, RWKV_WIDTH), F32)
    return pl.pallas_call(
        functools.partial(_scan_kernel, n_chunks, s0 is not None),
        grid=(nb, n_chunks),
        in_specs=in_specs,
        out_specs=[out_f, out_b, out_f, out_b, st_spec],
        out_shape=[tok, tok, tok, tok, jax.ShapeDtypeStruct((nb, 2, N_PAIRS, PAIR, PAIR), F32)],
        scratch_shapes=[pltpu.VMEM((2, N_PAIRS, PAIR, PAIR), F32)],
        compiler_params=_cparams("parallel", "arbitrary"),
        name="rwkv_scan",
    )(*args)


def _ev_out_kernel(n_tiles, pa_ref, pap_ref, pan_ref, zb_ref, yf_ref, yb_ref, bf_ref, bb_ref,
                   cw_ref, lw_ref, lb_ref, e_ref, wo_ref, x_ref, g_ref, o_ref):
    i = pl.program_id(1)
    W = CONV_WIDTH
    pa = pa_ref[0]
    u, gate_b, gate_c, z = pa[:, 0:W], pa[:, W:2 * W], pa[:, 2 * W:3 * W], pa[:, 3 * W:4 * W]
    cu = gate_c * u
    cu_p = pap_ref[0][:, 2 * W:3 * W] * pap_ref[0][:, 0:W]
    cu_n = pan_ref[0][:, 2 * W:3 * W] * pan_ref[0][:, 0:W]
    prev, nxt = _neighbours(cu, cu_p, cu_n, i == 0, i == n_tiles - 1)
    cw = cw_ref[...]
    conv = cw[0:1] * prev + cw[1:2] * cu + cw[2:3] * nxt
    a_out = gate_b * conv * _silu(z)

    e = e_ref[...]
    ysum = yf_ref[0] + yb_ref[0]
    inv_n = 1.0 / RWKV_HEAD
    mean = _segsum(ysum, e) * inv_n
    dlt = ysum - mean
    var = _segsum(dlt * dlt, e) * inv_n
    rw = dlt * lax.rsqrt(var + RWKV_GN_EPS) * lw_ref[...] + lb_ref[...] + (bf_ref[0] + bb_ref[0])
    b_out = rw * _silu(zb_ref[0])

    y = (jnp.dot(a_out.astype(BF16), wo_ref[:W, :], preferred_element_type=F32)
         + jnp.dot(b_out.astype(BF16), wo_ref[W:, :], preferred_element_type=F32))
    o_ref[0] = x_ref[0] + g_ref[0] * y


def _ev_out_call(pa, zb, yf, yb, bf, bb, conv_w, lnx_w, lnx_b, e_bf16, wo_bf16, x, gate, tm):
    nb, t, d = x.shape
    n_tiles = t // tm
    hb = tm // HALO
    n_hblk = t // HALO
    row = lambda b, i: (b, i, 0)
    prv = lambda b, i: (b, jnp.maximum(i * hb - 1, 0), 0)
    nxt = lambda b, i: (b, jnp.minimum((i + 1) * hb, n_hblk - 1), 0)
    vec = lambda b, i: (b, 0, 0)
    wspec = pl.BlockSpec((1, tm, RWKV_WIDTH), row)
    return pl.pallas_call(
        functools.partial(_ev_out_kernel, n_tiles),
        grid=(nb, n_tiles),
        in_specs=[pl.BlockSpec((1, tm, A_COLS), row),
                  pl.BlockSpec((1, HALO, A_COLS), prv),
                  pl.BlockSpec((1, HALO, A_COLS), nxt),
                  wspec, wspec, wspec, wspec, wspec,
                  _full(conv_w.shape), _full(lnx_w.shape), _full(lnx_b.shape),
                  _full(e_bf16.shape), _full(wo_bf16.shape),
                  pl.BlockSpec((1, tm, d), row),
                  pl.BlockSpec((1, 1, d), vec)],
        out_specs=pl.BlockSpec((1, tm, d), row),
        out_shape=jax.ShapeDtypeStruct((nb, t, d), F32),
        compiler_params=_cparams("parallel", "parallel"),
        name="ev_out",
    )(pa, pa, pa, zb, yf, yb, bf, bb, conv_w, lnx_w, lnx_b, e_bf16, wo_bf16, x, gate)


def _head_rms(t, g):
    return t * lax.rsqrt(jnp.mean(t * t, axis=-1, keepdims=True) + NORM_EPS) * g


def _keys_values(p_kv, kvn, wkvb, gkn, gkr, gkr_sw, cos, sin, k_ref, v_ref):
    kv_a = p_kv[:, :KV_LORA]
    kr = p_kv[:, KV_LORA:KV_LORA + QK_ROPE]
    kr_sw = p_kv[:, KV_LORA + QK_ROPE:KV_LORA + 2 * QK_ROPE]
    rs = lax.rsqrt(jnp.mean(kr * kr, axis=-1, keepdims=True) + NORM_EPS)
    if cos is None:
        k_rope = kr * rs * gkr
    else:
        k_rope = ((kr * gkr) * cos + (kr_sw * gkr_sw) * sin) * rs
    k_rope = k_rope.astype(BF16)
    kv = jnp.dot(_head_rms(kv_a, kvn).astype(BF16), wkvb, preferred_element_type=F32)
    for h in range(MLA_HEADS):
        kn = _head_rms(kv[:, h * QK_NOPE:(h + 1) * QK_NOPE], gkn)
        k_ref[0, h, :, 0:QK_NOPE] = kn.astype(BF16)
        k_ref[0, h, :, QK_NOPE:QK_DIM] = k_rope
        v_ref[0, h] = kv[:, MLA_WIDTH + h * V_HEAD:MLA_WIDTH + (h + 1) * V_HEAD].astype(BF16)


def _od_lat_kernel(x_ref, sc_ref, sh_ref, nw_ref, win_ref, qan_ref, kvn_ref, wqb_ref, wkvb_ref,
                   gqn_ref, gqr_ref, gqrs_ref, gkn_ref, gkr_ref, gkrs_ref, cos_ref, sin_ref,
                   kin_ref, vin_ref, q_ref, k_ref, v_ref, z_ref):
    del kin_ref, vin_ref
    h = _modulated_norm(x_ref[0], nw_ref[...], sc_ref[0], sh_ref[0]).astype(BF16)
    p = jnp.dot(h, win_ref[...], preferred_element_type=F32)
    kv_off = Q_LORA
    z_off = Q_LORA + KV_LORA + 2 * QK_ROPE
    z_ref[0] = p[:, z_off:]
    cos = cos_ref[...]
    sin = sin_ref[...]
    _keys_values(p[:, kv_off:z_off], kvn_ref[...], wkvb_ref[...], gkn_ref[...], gkr_ref[...], gkrs_ref[...],
                 cos[:, :QK_ROPE], sin[:, :QK_ROPE], k_ref, v_ref)

    q = jnp.dot(_head_rms(p[:, :Q_LORA], qan_ref[...]).astype(BF16), wqb_ref[...],
                preferred_element_type=F32)
    rw = MLA_HEADS * QK_ROPE
    qr = q[:, MLA_WIDTH:MLA_WIDTH + rw]
    qr_sw = q[:, MLA_WIDTH + rw:MLA_WIDTH + 2 * rw]
    rot = (qr * gqr_ref[...]) * cos + (qr_sw * gqrs_ref[...]) * sin
    for hd in range(MLA_HEADS):
        qn = _head_rms(q[:, hd * QK_NOPE:(hd + 1) * QK_NOPE], gqn_ref[...]) * Q_SCALE
        q_ref[0, hd, :, 0:QK_NOPE] = qn.astype(BF16)
        sl = slice(hd * QK_ROPE, (hd + 1) * QK_ROPE)
        t = qr[:, sl]
        rs = lax.rsqrt(jnp.mean(t * t, axis=-1, keepdims=True) + NORM_EPS) * Q_SCALE
        q_ref[0, hd, :, QK_NOPE:QK_DIM] = (rot[:, sl] * rs).astype(BF16)


def _od_ctx_kernel(x_ref, sc_ref, sh_ref, nw_ref, win_ref, kvn_ref, wkvb_ref,
                   gkn_ref, gkr_ref, gkrs_ref, k_ref, v_ref):
    h = _modulated_norm(x_ref[0], nw_ref[...], sc_ref[0], sh_ref[0]).astype(BF16)
    p = jnp.dot(h, win_ref[...], preferred_element_type=F32)
    _keys_values(p, kvn_ref[...], wkvb_ref[...], gkn_ref[...], gkr_ref[...], gkrs_ref[...],
                 None, None, k_ref, v_ref)


def _od_ctx_call(ctx, scale, shift, nw, win_kv, kvn, wkvb, gkn, gkr, gkrs, t_all, tm):
    nb, tc, d = ctx.shape
    row = lambda b, i: (b, i, 0)
    vec = lambda b, i: (b, 0, 0)
    head = lambda b, i: (b, 0, i, 0)
    params = [nw, win_kv, kvn, wkvb, gkn, gkr, gkrs]
    return pl.pallas_call(
        _od_ctx_kernel,
        grid=(nb, tc // tm),
        in_specs=[pl.BlockSpec((1, tm, d), row), pl.BlockSpec((1, 1, d), vec), pl.BlockSpec((1, 1, d), vec)]
                 + [_full(p.shape) for p in params],
        out_specs=[pl.BlockSpec((1, MLA_HEADS, tm, QK_DIM), head),
                   pl.BlockSpec((1, MLA_HEADS, tm, V_HEAD), head)],
        out_shape=[jax.ShapeDtypeStruct((nb, MLA_HEADS, t_all, QK_DIM), BF16),
                   jax.ShapeDtypeStruct((nb, MLA_HEADS, t_all, V_HEAD), BF16)],
        compiler_params=_cparams("parallel", "parallel"),
        name="od_in_ctx",
    )(ctx, scale, shift, *params)


def _od_lat_call(x, scale, shift, params, cos, sin, k_all, v_all, ctx_len, tm):
    nb, t, d = x.shape
    off = ctx_len // tm
    row = lambda b, i: (b, i, 0)
    vec = lambda b, i: (b, 0, 0)
    head_q = lambda b, i: (b, 0, i, 0)
    head_k = lambda b, i: (b, 0, i + off, 0)
    tab = lambda b, i: (i, 0)
    rw = MLA_HEADS * QK_ROPE
    n_in = 3 + len(params) + 2
    return pl.pallas_call(
        _od_lat_kernel,
        grid=(nb, t // tm),
        in_specs=[pl.BlockSpec((1, tm, d), row), pl.BlockSpec((1, 1, d), vec), pl.BlockSpec((1, 1, d), vec)]
                 + [_full(p.shape) for p in params]
                 + [pl.BlockSpec((tm, rw), tab), pl.BlockSpec((tm, rw), tab),
                    pl.BlockSpec(memory_space=pl.ANY), pl.BlockSpec(memory_space=pl.ANY)],
        out_specs=[pl.BlockSpec((1, MLA_HEADS, tm, QK_DIM), head_q),
                   pl.BlockSpec((1, MLA_HEADS, tm, QK_DIM), head_k),
                   pl.BlockSpec((1, MLA_HEADS, tm, V_HEAD), head_k),
                   pl.BlockSpec((1, tm, MLA_WIDTH), row)],
        out_shape=[jax.ShapeDtypeStruct((nb, MLA_HEADS, t, QK_DIM), BF16),
                   jax.ShapeDtypeStruct(k_all.shape, BF16),
                   jax.ShapeDtypeStruct(v_all.shape, BF16),
                   jax.ShapeDtypeStruct((nb, t, MLA_WIDTH), F32)],
        input_output_aliases={n_in: 1, n_in + 1: 2},
        compiler_params=_cparams("parallel", "parallel"),
        name="od_in_lat",
    )(x, scale, shift, *params, cos, sin, k_all, v_all)


ATTN_SUB = 512


def _attn_kernel(q_ref, k_ref, v_ref, o_ref):
    k = k_ref[0, 0]
    v1 = v_ref[0, 0]
    n_sub = q_ref.shape[2] // ATTN_SUB

    def scores(i):
        return lax.dot_general(q_ref[0, 0, i * ATTN_SUB:(i + 1) * ATTN_SUB, :], k, (((1,), (1,)), ((), ())),
                               preferred_element_type=F32)

    s = scores(0)
    for i in range(n_sub):
        s_next = scores(i + 1) if i + 1 < n_sub else None
        p = jnp.exp2(s - jnp.max(s, axis=-1, keepdims=True))
        l = jnp.sum(p, axis=-1, keepdims=True)
        o_ref[0, i * ATTN_SUB:(i + 1) * ATTN_SUB, :] = jnp.dot(p.astype(BF16), v1, preferred_element_type=F32) / l
        s = s_next


def _attn_call(q, k, v, tq):
    nb, nh, t, _ = q.shape
    tk = k.shape[2]
    return pl.pallas_call(
        _attn_kernel,
        grid=(nb, nh, t // tq),
        in_specs=[pl.BlockSpec((1, 1, tq, QK_DIM), lambda b, h, i: (b, h, i, 0)),
                  pl.BlockSpec((1, 1, tk, QK_DIM), lambda b, h, i: (b, h, 0, 0)),
                  pl.BlockSpec((1, 1, tk, V_HEAD), lambda b, h, i: (b, h, 0, 0))],
        out_specs=pl.BlockSpec((1, tq, V_HEAD), lambda b, h, i: (b, i, h)),
        out_shape=jax.ShapeDtypeStruct((nb, t, nh * V_HEAD), F32),
        compiler_params=_cparams("parallel", "parallel", "parallel"),
        name="mla_attn",
    )(q, k, v)


def _od_out_kernel(o_ref, z_ref, wo_ref, x_ref, g_ref, out_ref):
    a = (o_ref[0] * _silu(z_ref[0])).astype(BF16)
    out_ref[0] = x_ref[0] + g_ref[0] * jnp.dot(a, wo_ref[...], preferred_element_type=F32)


def _od_out_call(o, z, wo_bf16, x, gate, tm):
    nb, t, d = x.shape
    row = lambda b, i: (b, i, 0)
    return pl.pallas_call(
        _od_out_kernel,
        grid=(nb, t // tm),
        in_specs=[pl.BlockSpec((1, tm, MLA_WIDTH), row), pl.BlockSpec((1, tm, MLA_WIDTH), row),
                  _full(wo_bf16.shape), pl.BlockSpec((1, tm, d), row),
                  pl.BlockSpec((1, 1, d), lambda b, i: (b, 0, 0))],
        out_specs=pl.BlockSpec((1, tm, d), row),
        out_shape=jax.ShapeDtypeStruct((nb, t, d), F32),
        compiler_params=_cparams("parallel", "parallel"),
        name="od_out",
    )(o, z, wo_bf16, x, gate)


def _rope_tables(n):
    rows = n // GRID_W
    row = jnp.repeat(jnp.arange(rows, dtype=F32), GRID_W)
    col = jnp.tile(jnp.arange(GRID_W, dtype=F32), rows)
    inv_freq = ROPE_THETA ** (-jnp.arange(ROPE_PAIRS, dtype=F32) / ROPE_PAIRS)
    ar = row[:, None] * inv_freq
    ac = col[:, None] * inv_freq
    cos = jnp.concatenate([jnp.cos(ar), jnp.cos(ar), jnp.cos(ac), jnp.cos(ac)], axis=1)
    sin = jnp.concatenate([-jnp.sin(ar), jnp.sin(ar), -jnp.sin(ac), jnp.sin(ac)], axis=1)
    return cos, sin


def _row_tile(t, want):
    tm = min(t, want)
    assert t % tm == 0 and tm % HALO == 0
    return tm


def kernel(x, c, ctx, c_ctx, ada_w, ada_b, norm_w, ev_w_in, ev_conv_w, ev_mu, ev_k_k, ev_k_a, ev_w0, ev_w2, ev_a0, ev_a2, ev_r_k, ev_lnx_w, ev_lnx_b, ev_w_out, od_w_in, od_q_a_norm, od_kv_a_norm, od_w_qb, od_w_kvb, od_gq_nope, od_gq_rope, od_gk_nope, od_gk_rope, od_w_o):
    nb, t, d = x.shape
    tc = ctx.shape[1]
    assert t % CHUNK == 0 and tc % CHUNK == 0 and t % GRID_W == 0

    rows = -(-(nb + 1) // HALO) * HALO
    cc = jnp.zeros((rows, d), F32).at[:nb].set(c).at[nb].set(c_ctx)
    mod = _ada_call(cc, ada_w, ada_b)

    def mods(layer):
        m = mod[layer]
        lat = [m[:nb, i * d:(i + 1) * d].reshape(nb, 1, d) for i in range(3)]
        cx = [jnp.broadcast_to(m[nb, i * d:(i + 1) * d].reshape(1, 1, d), (nb, 1, d)) for i in range(3)]
        return lat, cx

    e_np = np.arange(PAIR)[:, None] // RWKV_HEAD == np.arange(PAIR)[None, :] // RWKV_HEAD
    e_bf16 = jnp.asarray(e_np, BF16)

    (sh_l, sc_l, g_l), (sh_c, sc_c, g_c) = mods(0)
    nw0 = norm_w[0].reshape(1, d)
    w_in0 = ev_w_in[0].astype(BF16)
    tm_l = _row_tile(t, 256)
    tm_c = _row_tile(tc, 256)
    pa_c, pr_c, zb_c = _ev_in_call(ctx, sc_c, sh_c, nw0, w_in0, tm_c)
    pa_l, pr_l, zb_l = _ev_in_call(x, sc_l, sh_l, nw0, w_in0, tm_l)

    W = RWKV_WIDTH
    vec = lambda a: a.reshape(1, -1)
    lora = jnp.zeros((2, 2 * LORA, 2 * W), F32)
    lora = lora.at[:, :LORA, :W].set(ev_w2[0]).at[:, LORA:, W:].set(ev_a2[0])
    lora_hi = lora.astype(BF16)
    lora_lo = (lora - lora_hi.astype(F32)).astype(BF16)
    wa0 = jnp.concatenate([ev_w0[0], ev_a0[0]], axis=-1).reshape(2, 1, 2 * W)
    params = [vec(1.0 - ev_mu[0]), vec(0.5 * ev_mu[0]), vec(ev_k_k[0]), vec(ev_k_a[0]), vec(1.0 - ev_k_a[0]),
              vec(ev_r_k[0]), wa0, lora_hi, lora_lo, e_bf16]
    yf_c, yb_c, bf_c, bb_c, s_c = _scan_call(pr_c, params, None)
    yf_l, yb_l, bf_l, bb_l, _ = _scan_call(pr_l, params, s_c)

    wo0 = ev_w_out[0].astype(BF16)
    lw, lb = vec(ev_lnx_w[0]), vec(ev_lnx_b[0])
    x1 = _ev_out_call(pa_l, zb_l, yf_l, yb_l, bf_l, bb_l, ev_conv_w[0], lw, lb, e_bf16, wo0, x, g_l, tm_l)
    ctx1 = _ev_out_call(pa_c, zb_c, yf_c, yb_c, bf_c, bb_c, ev_conv_w[0], lw, lb, e_bf16, wo0, ctx, g_c, tm_c)

    (sh_l, sc_l, g_l), (sh_c, sc_c, _) = mods(1)
    nw1 = norm_w[1].reshape(1, d)
    swap = np.arange(QK_ROPE) ^ ROPE_PAIRS
    hq = np.arange(MLA_HEADS)[:, None] * QK_DIM
    idx_q = np.concatenate([(hq + np.arange(QK_NOPE)).ravel(),
                            (hq + QK_NOPE + np.arange(QK_ROPE)).ravel(),
                            (hq + QK_NOPE + swap).ravel()])
    hk = np.arange(MLA_HEADS)[:, None] * (QK_NOPE + V_HEAD)
    idx_kv = np.concatenate([(hk + np.arange(QK_NOPE)).ravel(), (hk + QK_NOPE + np.arange(V_HEAD)).ravel()])
    r_off = Q_LORA + KV_LORA
    idx_in = np.concatenate([np.arange(r_off + QK_ROPE), r_off + swap,
                             np.arange(r_off + QK_ROPE, od_w_in.shape[2])])
    w_in1 = od_w_in[0][:, idx_in].astype(BF16)
    wqb = od_w_qb[0][:, idx_q].astype(BF16)
    wkvb = od_w_kvb[0][:, idx_kv].astype(BF16)
    gqr = jnp.tile(od_gq_rope[0], MLA_HEADS).reshape(1, -1)
    gqrs = jnp.tile(od_gq_rope[0][swap], MLA_HEADS).reshape(1, -1)
    gkr, gkrs = vec(od_gk_rope[0]), vec(od_gk_rope[0][swap])
    gqn, gkn = vec(od_gq_nope[0]), vec(od_gk_nope[0])
    qan, kvn = vec(od_q_a_norm[0]), vec(od_kv_a_norm[0])
    cos, sin = _rope_tables(t)
    cos = jnp.tile(cos, (1, MLA_HEADS))
    sin = jnp.tile(sin, (1, MLA_HEADS))

    tm_c = _row_tile(tc, 256)
    tm_l = _row_tile(t, tm_c)
    k_all, v_all = _od_ctx_call(ctx1, sc_c, sh_c, nw1, w_in1[:, Q_LORA:Q_LORA + KV_LORA + 2 * QK_ROPE],
                                kvn, wkvb, gkn, gkr, gkrs, tc + t, tm_c)
    params = [nw1, w_in1, qan, kvn, wqb, wkvb, gqn, gqr, gqrs, gkn, gkr, gkrs]
    q, k_all, v_all, z = _od_lat_call(x1, sc_l, sh_l, params, cos, sin, k_all, v_all, tc, tm_l)
    o = _attn_call(q, k_all, v_all, _row_tile(t, 2 * ATTN_SUB))
    return _od_out_call(o, z, od_w_o[0].astype(BF16), x1, g_l, _row_tile(t, 256))
```

```python
import functools

import numpy as np
import jax
import jax.numpy as jnp
from jax import lax
from jax.experimental import pallas as pl
from jax.experimental.pallas import tpu as pltpu

F32 = jnp.float32
BF16 = jnp.bfloat16

NORM_EPS = 1e-6
GRID_W = 64

CONV_WIDTH = 512
RWKV_HEAD = 64
RWKV_WIDTH = 512
LORA = 64
RWKV_GN_EPS = 64e-5
DECAY_SCALE = -float(np.exp(-0.5))
A_COLS = 4 * CONV_WIDTH
SHIFT_COLS = 3 * RWKV_WIDTH + 2 * LORA
CHUNK = 64
PAIR = 2 * RWKV_HEAD
N_PAIRS = RWKV_WIDTH // PAIR

MLA_HEADS = 8
QK_NOPE = 128
QK_ROPE = 64
QK_DIM = QK_NOPE + QK_ROPE
V_HEAD = 128
Q_LORA = 384
KV_LORA = 256
MLA_WIDTH = MLA_HEADS * V_HEAD
SM_SCALE = QK_DIM ** -0.5
LOG2E = 1.4426950408889634
Q_SCALE = SM_SCALE * LOG2E
ROPE_THETA = 10000.0
ROPE_PAIRS = QK_ROPE // 4

VMEM_LIMIT_BYTES = 56 * 1024 * 1024
HALO = 8


def _cparams(*sem):
    return pltpu.CompilerParams(dimension_semantics=sem, vmem_limit_bytes=VMEM_LIMIT_BYTES)


def _silu(x):
    return x / (1.0 + jnp.exp(-x))


def _dot(a, b):
    return jnp.dot(a.astype(BF16), b.astype(BF16), preferred_element_type=F32)


def _dot_nt(a, b):
    return lax.dot_general(a.astype(BF16), b.astype(BF16), (((1,), (1,)), ((), ())),
                           preferred_element_type=F32)


def _bf16_split(x):
    hi = x.astype(BF16)
    return hi, (x - hi.astype(F32)).astype(BF16)


def _mm(a, b):
    return jnp.dot(a, b, preferred_element_type=F32)


def _dot3(a, b_hi, b_lo):
    a_hi, a_lo = _bf16_split(a)
    return _mm(a_hi, b_hi) + (_mm(a_lo, b_hi) + _mm(a_hi, b_lo))


def _exact_lhs_dot(m_bf16, x):
    hi = x.astype(BF16)
    r1 = x - hi.astype(F32)
    mid = r1.astype(BF16)
    lo = (r1 - mid.astype(F32)).astype(BF16)
    return _mm(m_bf16, hi) + (_mm(m_bf16, mid) + _mm(m_bf16, lo))


def _segsum(x, e2):
    out = []
    for i in range(x.shape[1] // PAIR):
        hi, lo = _bf16_split(x[:, i * PAIR:(i + 1) * PAIR])
        out.append(_mm(hi, e2) + _mm(lo, e2))
    return jnp.concatenate(out, axis=1)


def _full(shape):
    return pl.BlockSpec(shape, lambda *_: (0,) * len(shape))


def _ada_kernel(c_ref, w_ref, b_ref, o_ref):
    o_ref[0] = _dot3(_silu(c_ref[...]), *_bf16_split(w_ref[0])) + b_ref[0]


def _ada_call(cc, ada_w, ada_b):
    depth, d, d3 = ada_w.shape
    rows = cc.shape[0]
    tn = 1024
    return pl.pallas_call(
        _ada_kernel,
        grid=(depth, d3 // tn),
        in_specs=[pl.BlockSpec((rows, d), lambda l, j: (0, 0)),
                  pl.BlockSpec((1, d, tn), lambda l, j: (l, 0, j)),
                  pl.BlockSpec((1, 1, tn), lambda l, j: (l, 0, j))],
        out_specs=pl.BlockSpec((1, rows, tn), lambda l, j: (l, 0, j)),
        out_shape=jax.ShapeDtypeStruct((depth, rows, d3), F32),
        compiler_params=_cparams("arbitrary", "arbitrary"),
        name="ada_mod",
    )(cc, ada_w, ada_b.reshape(depth, 1, d3))


def _modulated_norm(x, nw, scale, shift):
    rs = lax.rsqrt(jnp.mean(x * x, axis=-1, keepdims=True) + NORM_EPS)
    return (x * rs * nw) * (1.0 + scale) + shift


def _ev_in_kernel(x_ref, sc_ref, sh_ref, nw_ref, w_ref, pa_ref, pr_ref, zb_ref):
    h = _modulated_norm(x_ref[0], nw_ref[...], sc_ref[0], sh_ref[0]).astype(BF16)
    pa_ref[0] = jnp.dot(h, w_ref[:, :A_COLS], preferred_element_type=F32)
    pr_ref[0] = jnp.dot(h, w_ref[:, A_COLS:A_COLS + SHIFT_COLS], preferred_element_type=F32)
    zb_ref[0] = jnp.dot(h, w_ref[:, A_COLS + SHIFT_COLS:], preferred_element_type=F32)


def _ev_in_call(x, scale, shift, nw, w_bf16, tm):
    nb, t, d = x.shape
    ncol = w_bf16.shape[1]
    row = lambda b, i: (b, i, 0)
    vec = lambda b, i: (b, 0, 0)
    return pl.pallas_call(
        _ev_in_kernel,
        grid=(nb, t // tm),
        in_specs=[pl.BlockSpec((1, tm, d), row),
                  pl.BlockSpec((1, 1, d), vec),
                  pl.BlockSpec((1, 1, d), vec),
                  _full((1, d)),
                  _full((d, ncol))],
        out_specs=[pl.BlockSpec((1, tm, A_COLS), row),
                   pl.BlockSpec((1, tm, SHIFT_COLS), row),
                   pl.BlockSpec((1, tm, RWKV_WIDTH), row)],
        out_shape=[jax.ShapeDtypeStruct((nb, t, A_COLS), F32),
                   jax.ShapeDtypeStruct((nb, t, SHIFT_COLS), F32),
                   jax.ShapeDtypeStruct((nb, t, RWKV_WIDTH), F32)],
        compiler_params=_cparams("parallel", "parallel"),
        name="ev_in",
    )(x, scale, shift, nw, w_bf16)


def _neighbours(cur, prev_halo, next_halo, is_first, is_last):
    n = cur.shape[0]
    rowi = lax.broadcasted_iota(jnp.int32, cur.shape, 0)
    prev_row = jnp.where(is_first, 0.0, prev_halo[HALO - 1:HALO, :])
    next_row = jnp.where(is_last, 0.0, next_halo[0:1, :])
    prev = jnp.where(rowi == 0, prev_row, pltpu.roll(cur, 1, 0))
    nxt = jnp.where(rowi == n - 1, next_row, pltpu.roll(cur, n - 1, 0))
    return prev, nxt


OPERANDS = ("rh", "ah", "bc", "kc", "bt", "kt", "v", "bon")


def _scan_prep_pieces(reverse, cur_ref, prv_ref, nxt_ref, is_first, is_last, prm, w_dir, ops_ref, gc_ref, d):
    C = CHUNK
    W = RWKV_WIDTH
    mu_keep, mu_half, k_k, k_a, ka_keep, r_k, e2 = prm
    wa0, lora_hi, lora_lo = w_dir
    t = {}

    def shift():
        pr = cur_ref[0]
        prev, nxt = _neighbours(pr, prv_ref[0], nxt_ref[0], is_first, is_last)
        x = mu_keep * pr + mu_half * (prev + nxt)
        t.update(r=x[:, 0:W], k=x[:, W:2 * W], v=x[:, 2 * W:3 * W], wa=x[:, 3 * W:3 * W + 2 * LORA])

    def rates():
        lane = lax.broadcasted_iota(jnp.int32, (C, PAIR), 1)
        tw = jnp.where(lane < LORA, jnp.tanh(t["wa"]), t["wa"])
        lo = _dot3(tw, lora_hi, lora_lo)
        sig = 1.0 / (1.0 + jnp.exp(-(wa0 + lo)))
        t.update(lw=DECAY_SCALE * sig[:, :W], iclr=sig[:, W:])

    def keys():
        k, iclr = t["k"], t["iclr"]
        kkv = k * k_k
        kkn = kkv * lax.rsqrt(jnp.maximum(_segsum(kkv * kkv, e2), 1e-24))
        k_dir = k * (ka_keep + iclr * k_a)
        t.update(kkn=kkn, k_dir=k_dir, bvec=kkn * iclr)
        ops_ref[d, OPERANDS.index("bon")] = _segsum(t["r"] * k_dir * r_k, e2) * t["v"]

    def decays():
        lw = t["lw"]
        ti = lax.broadcasted_iota(jnp.int32, (C, C), 0)
        si = lax.broadcasted_iota(jnp.int32, (C, C), 1)
        tri = ((si >= ti) if reverse else (si <= ti)).astype(BF16)
        g_in = _exact_lhs_dot(tri, lw)
        g_tot = jnp.sum(lw, axis=0, keepdims=True)
        t.update(gam=jnp.exp(g_in), gam_ex=jnp.exp(g_in - lw), gam_inv=jnp.exp(-g_in), gam_c=jnp.exp(g_tot))

    def store():
        gam_c = t["gam_c"]
        bc = t["bvec"] * t["gam_inv"]
        kc = t["k_dir"] * t["gam_inv"]
        out = dict(rh=t["r"] * t["gam"], ah=-t["kkn"] * t["gam_ex"], bc=bc, kc=kc,
                   bt=bc * gam_c, kt=kc * gam_c, v=t["v"])
        for name, val in out.items():
            ops_ref[d, OPERANDS.index(name)] = val
        gc_ref[d] = gam_c

    return [shift, rates, keys, decays, store]


def _scan_kernel(n_chunks, has_s0, *refs):
    refs = list(refs)
    first_refs = (refs[0:3], refs[3:6])
    next_refs = (refs[6:9], refs[9:12])
    (muk_ref, muh_ref, kk_ref, ka_ref, kak_ref, rk_ref, wa0_ref, lh_ref, ll_ref, e_ref) = refs[12:22]
    pos = 22
    s0_ref = None
    if has_s0:
        s0_ref = refs[pos]
        pos += 1
    y_refs = refs[pos:pos + 2]
    bon_refs = refs[pos + 2:pos + 4]
    sfin_ref, st_ref, ops_a, gc_a, ops_b, gc_b = refs[pos + 4:pos + 10]

    C = CHUNK
    c = pl.program_id(1)
    prm = (muk_ref[...], muh_ref[...], kk_ref[...], ka_ref[...], kak_ref[...], rk_ref[...], e_ref[...])

    def pieces(d, src, is_first, is_last, ops_dst, gc_dst):
        cur, prv, nxt = src
        return _scan_prep_pieces(d == 1, cur, prv, nxt, is_first, is_last, prm,
                                 (wa0_ref[d], lh_ref[d], ll_ref[d]), ops_dst, gc_dst, d)

    @pl.when(c == 0)
    def _():
        if has_s0:
            st_ref[...] = s0_ref[0]
        else:
            st_ref[...] = jnp.zeros_like(st_ref)
        for piece in pieces(0, first_refs[0], True, n_chunks == 1, ops_a, gc_a) + \
                pieces(1, first_refs[1], n_chunks == 1, True, ops_a, gc_a):
            piece()

    nxt_f = jnp.minimum(c + 1, n_chunks - 1)
    nxt_b = jnp.maximum(n_chunks - 2 - c, 0)

    def chain(ops_ref, gc_ref, overlap):
        n2 = 2 * C
        ri = lax.broadcasted_iota(jnp.int32, (n2, n2), 0)
        ci = lax.broadcasted_iota(jnp.int32, (n2, n2), 1)
        same_head = (ri >= C) == (ci >= C)
        rt = ri & (C - 1)
        ct = ci & (C - 1)
        strict = (same_head & (ct < rt), same_head & (ct > rt))
        incl = (same_head & (ct <= rt), same_head & (ct >= rt))
        diag = ri == ci
        eye = diag.astype(F32)
        blocks = [same_head & ((rt >> s) == (ct >> s)) for s in range(1, C.bit_length())]
        first_head = lax.broadcasted_iota(jnp.int32, (C, PAIR), 1) < RWKV_HEAD

        def stk(z):
            return jnp.concatenate([jnp.where(first_head, z, 0.0), jnp.where(first_head, 0.0, z)], axis=0)

        def dup(z):
            return jnp.concatenate([z, z], axis=0)

        chains = [(d, slice(p * PAIR, (p + 1) * PAIR), p) for d in range(2) for p in range(N_PAIRS)]
        n_ch = len(chains)

        def operand(name, d, sl):
            return ops_ref[d, OPERANDS.index(name), :, sl]

        ah_s = [stk(operand("ah", d, sl)) for d, sl, _ in chains]
        rh_s = [stk(operand("rh", d, sl)) for d, sl, _ in chains]
        v_s = [stk(operand("v", d, sl)) for d, sl, _ in chains]
        pm = [_dot_nt(jnp.concatenate([ah_s[i], rh_s[i]], axis=0),
                      jnp.concatenate([dup(operand("bc", d, sl)), dup(operand("kc", d, sl))], axis=0))
              for i, (d, sl, _) in enumerate(chains)]
        overlap()
        a_ab = [jnp.where(strict[d], pm[i][:n2, :n2], 0.0) for i, (d, _, _) in enumerate(chains)]
        a_ak = [jnp.where(strict[d], pm[i][:n2, n2:], 0.0) for i, (d, _, _) in enumerate(chains)]
        a_r = [jnp.concatenate([jnp.where(incl[d], pm[i][n2:, :n2], 0.0),
                                jnp.where(incl[d], pm[i][n2:, n2:], 0.0)], axis=1)
               for i, (d, _, _) in enumerate(chains)]
        akv = [_dot(a_ak[i], v_s[i]) for i in range(n_ch)]
        tinv = [jnp.where(blocks[0], a, 0.0) + eye for a in a_ab]
        overlap()
        for lvl in range(1, len(blocks)):
            off = blocks[lvl] & ~blocks[lvl - 1]
            t1 = [_dot(jnp.where(off, a_ab[i], 0.0), tinv[i]) for i in range(n_ch)]
            overlap()
            t2 = [_dot(tinv[i], t1[i]) for i in range(n_ch)]
            tinv = [tinv[i] + t2[i] for i in range(n_ch)]
            overlap()
        xx = [_dot(tinv[i], jnp.concatenate([ah_s[i], akv[i]], axis=1)) for i in range(n_ch)]
        zeros = jnp.zeros((n2, PAIR), F32)
        rr = []
        for i, (d, sl, _) in enumerate(chains):
            rhs = jnp.concatenate([xx[i], jnp.concatenate([zeros, v_s[i]], axis=1)], axis=0)
            lhs = jnp.concatenate(
                [a_r[i], jnp.concatenate([stk(operand("bt", d, sl)).T, stk(operand("kt", d, sl)).T], axis=1)], axis=0)
            rr.append(_dot(lhs, rhs))
        for i, (d, sl, p) in enumerate(chains):
            q_t = rh_s[i] + rr[i][:n2, :PAIR]
            st = st_ref[d, p]
            st_hi, st_lo = _bf16_split(st)
            m_off = rr[i][n2:, :PAIR].astype(BF16)
            ys = _mm(q_t.astype(BF16), st_hi) + rr[i][:n2, PAIR:]
            g_col = jnp.sum(jnp.where(diag, gc_ref[d][:, sl], 0.0), axis=1, keepdims=True)
            st_ref[d, p] = (g_col * st + (_mm(m_off, st_hi) + _mm(m_off, st_lo))) + rr[i][n2:, PAIR:]
            y_refs[d][0, :, sl] = ys[:C] + ys[C:]
        for d in range(2):
            bon_refs[d][0] = ops_ref[d, OPERANDS.index("bon")]

    def step(ops_ref, gc_ref, ops_nxt, gc_nxt):
        prep_f = pieces(0, next_refs[0], nxt_f == 0, nxt_f == n_chunks - 1, ops_nxt, gc_nxt)
        prep_b = pieces(1, next_refs[1], nxt_b == 0, nxt_b == n_chunks - 1, ops_nxt, gc_nxt)
        background = [fn for pair in zip(prep_f, prep_b) for fn in pair]

        def overlap():
            if background:
                background.pop(0)()

        chain(ops_ref, gc_ref, overlap)
        while background:
            overlap()

    @pl.when(c % 2 == 0)
    def _():
        step(ops_a, gc_a, ops_b, gc_b)

    @pl.when(c % 2 == 1)
    def _():
        step(ops_b, gc_b, ops_a, gc_a)

    @pl.when(c == n_chunks - 1)
    def _():
        sfin_ref[0] = st_ref[...]


def _scan_call(pr, params, s0):
    nb, t, _ = pr.shape
    C = CHUNK
    n_chunks = t // C
    hb = C // HALO
    n_hblk = t // HALO

    def specs(cs_of):
        return [pl.BlockSpec((1, C, SHIFT_COLS), lambda b, c: (b, cs_of(c), 0)),
                pl.BlockSpec((1, HALO, SHIFT_COLS), lambda b, c: (b, jnp.maximum(cs_of(c) * hb - 1, 0), 0)),
                pl.BlockSpec((1, HALO, SHIFT_COLS), lambda b, c: (b, jnp.minimum((cs_of(c) + 1) * hb, n_hblk - 1), 0))]

    first_f = lambda c: 0
    first_b = lambda c: n_chunks - 1
    next_f = lambda c: jnp.minimum(c + 1, n_chunks - 1)
    next_b = lambda c: jnp.maximum(n_chunks - 2 - c, 0)
    st_spec = pl.BlockSpec((1, 2, N_PAIRS, PAIR, PAIR), lambda b, c: (b, 0, 0, 0, 0))
    in_specs = specs(first_f) + specs(first_b) + specs(next_f) + specs(next_b) + [_full(p.shape) for p in params]
    args = [pr] * 12 + list(params)
    if s0 is not None:
        in_specs.append(st_spec)
        args.append(s0)
    out_f = pl.BlockSpec((1, C, RWKV_WIDTH), lambda b, c: (b, c, 0))
    out_b = pl.BlockSpec((1, C, RWKV_WIDTH), lambda b, c: (b, n_chunks - 1 - c, 0))
    tok = jax.ShapeDtypeStruct((nb, t, RWKV_WIDTH), F32)
    return pl.pallas_call(
        functools.partial(_scan_kernel, n_chunks, s0 is not None),
        grid=(nb, n_chunks),
        in_specs=in_specs,
        out_specs=[out_f, out_b, out_f, out_b, st_spec],
        out_shape=[tok, tok, tok, tok, jax.ShapeDtypeStruct((nb, 2, N_PAIRS, PAIR, PAIR), F32)],
        scratch_shapes=[pltpu.VMEM((2, N_PAIRS, PAIR, PAIR), F32),
                        pltpu.VMEM((2, len(OPERANDS), C, RWKV_WIDTH), F32), pltpu.VMEM((2, 1, RWKV_WIDTH), F32),
                        pltpu.VMEM((2, len(OPERANDS), C, RWKV_WIDTH), F32), pltpu.VMEM((2, 1, RWKV_WIDTH), F32)],
        compiler_params=_cparams("parallel", "arbitrary"),
        name="rwkv_scan",
    )(*args)


def _ev_out_kernel(n_tiles, pa_ref, pap_ref, pan_ref, zb_ref, yf_ref, yb_ref, bf_ref, bb_ref,
                   cw_ref, lw_ref, lb_ref, e_ref, wo_ref, x_ref, g_ref, o_ref):
    i = pl.program_id(1)
    W = CONV_WIDTH
    pa = pa_ref[0]
    u, gate_b, gate_c, z = pa[:, 0:W], pa[:, W:2 * W], pa[:, 2 * W:3 * W], pa[:, 3 * W:4 * W]
    cu = gate_c * u
    cu_p = pap_ref[0][:, 2 * W:3 * W] * pap_ref[0][:, 0:W]
    cu_n = pan_ref[0][:, 2 * W:3 * W] * pan_ref[0][:, 0:W]
    prev, nxt = _neighbours(cu, cu_p, cu_n, i == 0, i == n_tiles - 1)
    cw = cw_ref[...]
    conv = cw[0:1] * prev + cw[1:2] * cu + cw[2:3] * nxt
    a_out = gate_b * conv * _silu(z)

    e = e_ref[...]
    ysum = yf_ref[0] + yb_ref[0]
    inv_n = 1.0 / RWKV_HEAD
    mean = _segsum(ysum, e) * inv_n
    dlt = ysum - mean
    var = _segsum(dlt * dlt, e) * inv_n
    rw = dlt * lax.rsqrt(var + RWKV_GN_EPS) * lw_ref[...] + lb_ref[...] + (bf_ref[0] + bb_ref[0])
    b_out = rw * _silu(zb_ref[0])

    y = (jnp.dot(a_out.astype(BF16), wo_ref[:W, :], preferred_element_type=F32)
         + jnp.dot(b_out.astype(BF16), wo_ref[W:, :], preferred_element_type=F32))
    o_ref[0] = x_ref[0] + g_ref[0] * y


def _ev_out_call(pa, zb, yf, yb, bf, bb, conv_w, lnx_w, lnx_b, e_bf16, wo_bf16, x, gate, tm):
    nb, t, d = x.shape
    n_tiles = t // tm
    hb = tm // HALO
    n_hblk = t // HALO
    row = lambda b, i: (b, i, 0)
    prv = lambda b, i: (b, jnp.maximum(i * hb - 1, 0), 0)
    nxt = lambda b, i: (b, jnp.minimum((i + 1) * hb, n_hblk - 1), 0)
    vec = lambda b, i: (b, 0, 0)
    wspec = pl.BlockSpec((1, tm, RWKV_WIDTH), row)
    return pl.pallas_call(
        functools.partial(_ev_out_kernel, n_tiles),
        grid=(nb, n_tiles),
        in_specs=[pl.BlockSpec((1, tm, A_COLS), row),
                  pl.BlockSpec((1, HALO, A_COLS), prv),
                  pl.BlockSpec((1, HALO, A_COLS), nxt),
                  wspec, wspec, wspec, wspec, wspec,
                  _full(conv_w.shape), _full(lnx_w.shape), _full(lnx_b.shape),
                  _full(e_bf16.shape), _full(wo_bf16.shape),
                  pl.BlockSpec((1, tm, d), row),
                  pl.BlockSpec((1, 1, d), vec)],
        out_specs=pl.BlockSpec((1, tm, d), row),
        out_shape=jax.ShapeDtypeStruct((nb, t, d), F32),
        compiler_params=_cparams("parallel", "parallel"),
        name="ev_out",
    )(pa, pa, pa, zb, yf, yb, bf, bb, conv_w, lnx_w, lnx_b, e_bf16, wo_bf16, x, gate)


def _head_rms(t, g):
    return t * lax.rsqrt(jnp.mean(t * t, axis=-1, keepdims=True) + NORM_EPS) * g


def _keys_values(p_kv, kvn, wkvb, gkn, gkr, gkr_sw, cos, sin, k_ref, v_ref):
    kv_a = p_kv[:, :KV_LORA]
    kr = p_kv[:, KV_LORA:KV_LORA + QK_ROPE]
    kr_sw = p_kv[:, KV_LORA + QK_ROPE:KV_LORA + 2 * QK_ROPE]
    rs = lax.rsqrt(jnp.mean(kr * kr, axis=-1, keepdims=True) + NORM_EPS)
    if cos is None:
        k_rope = kr * rs * gkr
    else:
        k_rope = ((kr * gkr) * cos + (kr_sw * gkr_sw) * sin) * rs
    k_rope = k_rope.astype(BF16)
    kv = jnp.dot(_head_rms(kv_a, kvn).astype(BF16), wkvb, preferred_element_type=F32)
    for h in range(MLA_HEADS):
        kn = _head_rms(kv[:, h * QK_NOPE:(h + 1) * QK_NOPE], gkn)
        k_ref[0, h, :, 0:QK_NOPE] = kn.astype(BF16)
        k_ref[0, h, :, QK_NOPE:QK_DIM] = k_rope
        v_ref[0, h] = kv[:, MLA_WIDTH + h * V_HEAD:MLA_WIDTH + (h + 1) * V_HEAD].astype(BF16)


def _od_lat_kernel(x_ref, sc_ref, sh_ref, nw_ref, win_ref, qan_ref, kvn_ref, wqb_ref, wkvb_ref,
                   gqn_ref, gqr_ref, gqrs_ref, gkn_ref, gkr_ref, gkrs_ref, cos_ref, sin_ref,
                   kc_ref, vc_ref, q_ref, k_ref, v_ref, z_ref, *, n_ctx_tiles):
    i = pl.program_id(1)

    @pl.when(i < n_ctx_tiles)
    def _():
        k_ref[...] = kc_ref[...]
        v_ref[...] = vc_ref[...]

    @pl.when(i >= n_ctx_tiles)
    def _():
        _od_lat_tile(x_ref, sc_ref, sh_ref, nw_ref, win_ref, qan_ref, kvn_ref, wqb_ref, wkvb_ref,
                     gqn_ref, gqr_ref, gqrs_ref, gkn_ref, gkr_ref, gkrs_ref, cos_ref, sin_ref,
                     q_ref, k_ref, v_ref, z_ref)


def _od_lat_tile(x_ref, sc_ref, sh_ref, nw_ref, win_ref, qan_ref, kvn_ref, wqb_ref, wkvb_ref,
                 gqn_ref, gqr_ref, gqrs_ref, gkn_ref, gkr_ref, gkrs_ref, cos_ref, sin_ref,
                 q_ref, k_ref, v_ref, z_ref):
    h = _modulated_norm(x_ref[0], nw_ref[...], sc_ref[0], sh_ref[0]).astype(BF16)
    p = jnp.dot(h, win_ref[...], preferred_element_type=F32)
    kv_off = Q_LORA
    z_off = Q_LORA + KV_LORA + 2 * QK_ROPE
    z_ref[0] = p[:, z_off:]
    cos = cos_ref[...]
    sin = sin_ref[...]
    _keys_values(p[:, kv_off:z_off], kvn_ref[...], wkvb_ref[...], gkn_ref[...], gkr_ref[...], gkrs_ref[...],
                 cos[:, :QK_ROPE], sin[:, :QK_ROPE], k_ref, v_ref)

    q = jnp.dot(_head_rms(p[:, :Q_LORA], qan_ref[...]).astype(BF16), wqb_ref[...],
                preferred_element_type=F32)
    rw = MLA_HEADS * QK_ROPE
    qr = q[:, MLA_WIDTH:MLA_WIDTH + rw]
    qr_sw = q[:, MLA_WIDTH + rw:MLA_WIDTH + 2 * rw]
    rot = (qr * gqr_ref[...]) * cos + (qr_sw * gqrs_ref[...]) * sin
    for hd in range(MLA_HEADS):
        qn = _head_rms(q[:, hd * QK_NOPE:(hd + 1) * QK_NOPE], gqn_ref[...]) * Q_SCALE
        q_ref[0, hd, :, 0:QK_NOPE] = qn.astype(BF16)
        sl = slice(hd * QK_ROPE, (hd + 1) * QK_ROPE)
        t = qr[:, sl]
        rs = lax.rsqrt(jnp.mean(t * t, axis=-1, keepdims=True) + NORM_EPS) * Q_SCALE
        q_ref[0, hd, :, QK_NOPE:QK_DIM] = (rot[:, sl] * rs).astype(BF16)


def _od_ctx_kernel(x_ref, sc_ref, sh_ref, nw_ref, win_ref, kvn_ref, wkvb_ref,
                   gkn_ref, gkr_ref, gkrs_ref, k_ref, v_ref):
    h = _modulated_norm(x_ref[0], nw_ref[...], sc_ref[0], sh_ref[0]).astype(BF16)
    p = jnp.dot(h, win_ref[...], preferred_element_type=F32)
    _keys_values(p, kvn_ref[...], wkvb_ref[...], gkn_ref[...], gkr_ref[...], gkrs_ref[...],
                 None, None, k_ref, v_ref)


def _od_ctx_call(ctx, scale, shift, nw, win_kv, kvn, wkvb, gkn, gkr, gkrs, tm):
    nb, tc, d = ctx.shape
    row = lambda b, i: (b, i, 0)
    vec = lambda b, i: (b, 0, 0)
    head = lambda b, i: (b, 0, i, 0)
    params = [nw, win_kv, kvn, wkvb, gkn, gkr, gkrs]
    return pl.pallas_call(
        _od_ctx_kernel,
        grid=(nb, tc // tm),
        in_specs=[pl.BlockSpec((1, tm, d), row), pl.BlockSpec((1, 1, d), vec), pl.BlockSpec((1, 1, d), vec)]
                 + [_full(p.shape) for p in params],
        out_specs=[pl.BlockSpec((1, MLA_HEADS, tm, QK_DIM), head),
                   pl.BlockSpec((1, MLA_HEADS, tm, V_HEAD), head)],
        out_shape=[jax.ShapeDtypeStruct((nb, MLA_HEADS, tc, QK_DIM), BF16),
                   jax.ShapeDtypeStruct((nb, MLA_HEADS, tc, V_HEAD), BF16)],
        compiler_params=_cparams("parallel", "parallel"),
        name="od_in_ctx",
    )(ctx, scale, shift, *params)


def _od_lat_call(x, scale, shift, params, cos, sin, k_ctx, v_ctx, tm):
    nb, t, d = x.shape
    tc = k_ctx.shape[2]
    assert tc % tm == 0
    n_off = tc // tm
    lat = lambda i: jnp.maximum(i - n_off, 0)
    cxt = lambda i: jnp.minimum(i, n_off - 1)
    row = lambda b, i: (b, lat(i), 0)
    vec = lambda b, i: (b, 0, 0)
    head_q = lambda b, i: (b, 0, lat(i), 0)
    head_c = lambda b, i: (b, 0, cxt(i), 0)
    head_k = lambda b, i: (b, 0, i, 0)
    tab = lambda b, i: (lat(i), 0)
    rw = MLA_HEADS * QK_ROPE
    return pl.pallas_call(
        functools.partial(_od_lat_kernel, n_ctx_tiles=n_off),
        grid=(nb, n_off + t // tm),
        in_specs=[pl.BlockSpec((1, tm, d), row), pl.BlockSpec((1, 1, d), vec), pl.BlockSpec((1, 1, d), vec)]
                 + [_full(p.shape) for p in params]
                 + [pl.BlockSpec((tm, rw), tab), pl.BlockSpec((tm, rw), tab),
                    pl.BlockSpec((1, MLA_HEADS, tm, QK_DIM), head_c),
                    pl.BlockSpec((1, MLA_HEADS, tm, V_HEAD), head_c)],
        out_specs=[pl.BlockSpec((1, MLA_HEADS, tm, QK_DIM), head_q),
                   pl.BlockSpec((1, MLA_HEADS, tm, QK_DIM), head_k),
                   pl.BlockSpec((1, MLA_HEADS, tm, V_HEAD), head_k),
                   pl.BlockSpec((1, tm, MLA_WIDTH), row)],
        out_shape=[jax.ShapeDtypeStruct((nb, MLA_HEADS, t, QK_DIM), BF16),
                   jax.ShapeDtypeStruct((nb, MLA_HEADS, tc + t, QK_DIM), BF16),
                   jax.ShapeDtypeStruct((nb, MLA_HEADS, tc + t, V_HEAD), BF16),
                   jax.ShapeDtypeStruct((nb, t, MLA_WIDTH), F32)],
        compiler_params=_cparams("parallel", "arbitrary"),
        name="od_in_lat",
    )(x, scale, shift, *params, cos, sin, k_ctx, v_ctx)


ATTN_SUB = 512


def _attn_kernel(q_ref, k_ref, v_ref, o_ref):
    k = k_ref[0, 0]
    v = v_ref[0, 0]
    sub = min(ATTN_SUB, q_ref.shape[2])
    n_sub = q_ref.shape[2] // sub

    def scores(i):
        return lax.dot_general(q_ref[0, 0, i * sub:(i + 1) * sub, :], k, (((1,), (1,)), ((), ())),
                               preferred_element_type=F32)

    s = scores(0)
    for i in range(n_sub):
        s_next = scores(i + 1) if i + 1 < n_sub else None
        p = jnp.exp2(s - jnp.max(s, axis=-1, keepdims=True))
        l = jnp.sum(p, axis=-1, keepdims=True)
        o_ref[0, i * sub:(i + 1) * sub, :] = jnp.dot(p.astype(BF16), v, preferred_element_type=F32) / l
        s = s_next


def _attn_call(q, k, v, tq):
    nb, nh, t, _ = q.shape
    tk = k.shape[2]
    whole = lambda b, h, i: (b, h, 0, 0)
    return pl.pallas_call(
        _attn_kernel,
        grid=(nb, nh, t // tq),
        in_specs=[pl.BlockSpec((1, 1, tq, QK_DIM), lambda b, h, i: (b, h, i, 0)),
                  pl.BlockSpec((1, 1, tk, QK_DIM), whole),
                  pl.BlockSpec((1, 1, tk, V_HEAD), whole)],
        out_specs=pl.BlockSpec((1, tq, V_HEAD), lambda b, h, i: (b, i, h)),
        out_shape=jax.ShapeDtypeStruct((nb, t, nh * V_HEAD), F32),
        compiler_params=_cparams("parallel", "parallel", "parallel"),
        name="mla_attn",
    )(q, k, v)


def _od_out_kernel(o_ref, z_ref, wo_ref, x_ref, g_ref, out_ref):
    a = (o_ref[0] * _silu(z_ref[0])).astype(BF16)
    out_ref[0] = x_ref[0] + g_ref[0] * jnp.dot(a, wo_ref[...], preferred_element_type=F32)


def _od_out_call(o, z, wo_bf16, x, gate, tm):
    nb, t, d = x.shape
    row = lambda b, i: (b, i, 0)
    return pl.pallas_call(
        _od_out_kernel,
        grid=(nb, t // tm),
        in_specs=[pl.BlockSpec((1, tm, MLA_WIDTH), row), pl.BlockSpec((1, tm, MLA_WIDTH), row),
                  _full(wo_bf16.shape), pl.BlockSpec((1, tm, d), row),
                  pl.BlockSpec((1, 1, d), lambda b, i: (b, 0, 0))],
        out_specs=pl.BlockSpec((1, tm, d), row),
        out_shape=jax.ShapeDtypeStruct((nb, t, d), F32),
        compiler_params=_cparams("parallel", "parallel"),
        name="od_out",
    )(o, z, wo_bf16, x, gate)


def _rope_tables(n):
    rows = n // GRID_W
    row = jnp.repeat(jnp.arange(rows, dtype=F32), GRID_W)
    col = jnp.tile(jnp.arange(GRID_W, dtype=F32), rows)
    inv_freq = ROPE_THETA ** (-jnp.arange(ROPE_PAIRS, dtype=F32) / ROPE_PAIRS)
    ar = row[:, None] * inv_freq
    ac = col[:, None] * inv_freq
    cos = jnp.concatenate([jnp.cos(ar), jnp.cos(ar), jnp.cos(ac), jnp.cos(ac)], axis=1)
    sin = jnp.concatenate([-jnp.sin(ar), jnp.sin(ar), -jnp.sin(ac), jnp.sin(ac)], axis=1)
    return cos, sin


def _row_tile(t, want):
    tm = min(t, want)
    assert t % tm == 0 and tm % HALO == 0
    return tm


def kernel(x, c, ctx, c_ctx, ada_w, ada_b, norm_w, ev_w_in, ev_conv_w, ev_mu, ev_k_k, ev_k_a, ev_w0, ev_w2, ev_a0, ev_a2, ev_r_k, ev_lnx_w, ev_lnx_b, ev_w_out, od_w_in, od_q_a_norm, od_kv_a_norm, od_w_qb, od_w_kvb, od_gq_nope, od_gq_rope, od_gk_nope, od_gk_rope, od_w_o):
    nb, t, d = x.shape
    tc = ctx.shape[1]
    assert t % CHUNK == 0 and tc % CHUNK == 0 and t % GRID_W == 0

    rows = -(-(nb + 1) // HALO) * HALO
    cc = jnp.zeros((rows, d), F32).at[:nb].set(c).at[nb].set(c_ctx)
    mod = _ada_call(cc, ada_w, ada_b)

    def mods(layer):
        m = mod[layer]
        lat = [m[:nb, i * d:(i + 1) * d].reshape(nb, 1, d) for i in range(3)]
        cx = [jnp.broadcast_to(m[nb, i * d:(i + 1) * d].reshape(1, 1, d), (nb, 1, d)) for i in range(3)]
        return lat, cx

    e_np = np.arange(PAIR)[:, None] // RWKV_HEAD == np.arange(PAIR)[None, :] // RWKV_HEAD
    e_bf16 = jnp.asarray(e_np, BF16)

    (sh_l, sc_l, g_l), (sh_c, sc_c, g_c) = mods(0)
    nw0 = norm_w[0].reshape(1, d)
    w_in0 = ev_w_in[0].astype(BF16)
    tm_l = _row_tile(t, 256)
    tm_c = _row_tile(tc, 256)
    pa_c, pr_c, zb_c = _ev_in_call(ctx, sc_c, sh_c, nw0, w_in0, tm_c)
    pa_l, pr_l, zb_l = _ev_in_call(x, sc_l, sh_l, nw0, w_in0, tm_l)

    W = RWKV_WIDTH
    vec = lambda a: a.reshape(1, -1)
    lora = jnp.zeros((2, 2 * LORA, 2 * W), F32)
    lora = lora.at[:, :LORA, :W].set(ev_w2[0]).at[:, LORA:, W:].set(ev_a2[0])
    lora_hi = lora.astype(BF16)
    lora_lo = (lora - lora_hi.astype(F32)).astype(BF16)
    wa0 = jnp.concatenate([ev_w0[0], ev_a0[0]], axis=-1).reshape(2, 1, 2 * W)
    params = [vec(1.0 - ev_mu[0]), vec(0.5 * ev_mu[0]), vec(ev_k_k[0]), vec(ev_k_a[0]), vec(1.0 - ev_k_a[0]),
              vec(ev_r_k[0]), wa0, lora_hi, lora_lo, e_bf16]
    yf_c, yb_c, bf_c, bb_c, s_c = _scan_call(pr_c, params, None)
    yf_l, yb_l, bf_l, bb_l, _ = _scan_call(pr_l, params, s_c)

    wo0 = ev_w_out[0].astype(BF16)
    lw, lb = vec(ev_lnx_w[0]), vec(ev_lnx_b[0])
    x1 = _ev_out_call(pa_l, zb_l, yf_l, yb_l, bf_l, bb_l, ev_conv_w[0], lw, lb, e_bf16, wo0, x, g_l, tm_l)
    ctx1 = _ev_out_call(pa_c, zb_c, yf_c, yb_c, bf_c, bb_c, ev_conv_w[0], lw, lb, e_bf16, wo0, ctx, g_c, tm_c)

    (sh_l, sc_l, g_l), (sh_c, sc_c, _) = mods(1)
    nw1 = norm_w[1].reshape(1, d)
    swap = np.arange(QK_ROPE) ^ ROPE_PAIRS
    hq = np.arange(MLA_HEADS)[:, None] * QK_DIM
    idx_q = np.concatenate([(hq + np.arange(QK_NOPE)).ravel(),
                            (hq + QK_NOPE + np.arange(QK_ROPE)).ravel(),
                            (hq + QK_NOPE + swap).ravel()])
    hk = np.arange(MLA_HEADS)[:, None] * (QK_NOPE + V_HEAD)
    idx_kv = np.concatenate([(hk + np.arange(QK_NOPE)).ravel(), (hk + QK_NOPE + np.arange(V_HEAD)).ravel()])
    r_off = Q_LORA + KV_LORA
    idx_in = np.concatenate([np.arange(r_off + QK_ROPE), r_off + swap,
                             np.arange(r_off + QK_ROPE, od_w_in.shape[2])])
    w_in1 = od_w_in[0][:, idx_in].astype(BF16)
    wqb = od_w_qb[0][:, idx_q].astype(BF16)
    wkvb = od_w_kvb[0][:, idx_kv].astype(BF16)
    gqr = jnp.tile(od_gq_rope[0], MLA_HEADS).reshape(1, -1)
    gqrs = jnp.tile(od_gq_rope[0][swap], MLA_HEADS).reshape(1, -1)
    gkr, gkrs = vec(od_gk_rope[0]), vec(od_gk_rope[0][swap])
    gqn, gkn = vec(od_gq_nope[0]), vec(od_gk_nope[0])
    qan, kvn = vec(od_q_a_norm[0]), vec(od_kv_a_norm[0])
    cos, sin = _rope_tables(t)
    cos = jnp.tile(cos, (1, MLA_HEADS))
    sin = jnp.tile(sin, (1, MLA_HEADS))

    k_c, v_c = _od_ctx_call(ctx1, sc_c, sh_c, nw1, w_in1[:, Q_LORA:Q_LORA + KV_LORA + 2 * QK_ROPE],
                            kvn, wkvb, gkn, gkr, gkrs, _row_tile(tc, 256))
    params = [nw1, w_in1, qan, kvn, wqb, wkvb, gqn, gqr, gqrs, gkn, gkr, gkrs]
    q, k_all, v_all, z = _od_lat_call(x1, sc_l, sh_l, params, cos, sin, k_c, v_c, _row_tile(tc, 256))
    o = _attn_call(q, k_all, v_all, _row_tile(t, 2 * ATTN_SUB))
    return _od_out_call(o, z, od_w_o[0].astype(BF16), x1, g_l, _row_tile(t, 256))
```

```python
import functools

import numpy as np
import jax
import jax.numpy as jnp
from jax import lax
from jax.experimental import pallas as pl
from jax.experimental.pallas import tpu as pltpu

F32 = jnp.float32
BF16 = jnp.bfloat16

NORM_EPS = 1e-6
GRID_W = 64

CONV_WIDTH = 512
RWKV_HEAD = 64
RWKV_WIDTH = 512
LORA = 64
RWKV_GN_EPS = 64e-5
DECAY_SCALE = -float(np.exp(-0.5))
A_COLS = 4 * CONV_WIDTH
SHIFT_COLS = 3 * RWKV_WIDTH + 2 * LORA
CHUNK = 64
PAIR = 2 * RWKV_HEAD
N_PAIRS = RWKV_WIDTH // PAIR

MLA_HEADS = 8
QK_NOPE = 128
QK_ROPE = 64
QK_DIM = QK_NOPE + QK_ROPE
V_HEAD = 128
Q_LORA = 384
KV_LORA = 256
MLA_WIDTH = MLA_HEADS * V_HEAD
SM_SCALE = QK_DIM ** -0.5
LOG2E = 1.4426950408889634
Q_SCALE = SM_SCALE * LOG2E
V_COLS = 2 * V_HEAD
ROPE_THETA = 10000.0
ROPE_PAIRS = QK_ROPE // 4

VMEM_LIMIT_BYTES = 56 * 1024 * 1024
HALO = 8
HALO_BF16 = 16


def _cparams(*sem):
    return pltpu.CompilerParams(dimension_semantics=sem, vmem_limit_bytes=VMEM_LIMIT_BYTES)


def _silu(x):
    return x / (1.0 + jnp.exp(-x))


def _dot(a, b):
    return jnp.dot(a.astype(BF16), b.astype(BF16), preferred_element_type=F32)


def _dot_nt(a, b):
    return lax.dot_general(a.astype(BF16), b.astype(BF16), (((1,), (1,)), ((), ())),
                           preferred_element_type=F32)


def _bf16_split(x):
    hi = x.astype(BF16)
    return hi, (x - hi.astype(F32)).astype(BF16)


def _mm(a, b):
    return jnp.dot(a, b, preferred_element_type=F32)


def _dot3(a, b_hi, b_lo):
    a_hi, a_lo = _bf16_split(a)
    return _mm(a_hi, b_hi) + (_mm(a_lo, b_hi) + _mm(a_hi, b_lo))


def _exact_lhs_dot(m_bf16, x):
    hi = x.astype(BF16)
    r1 = x - hi.astype(F32)
    mid = r1.astype(BF16)
    lo = (r1 - mid.astype(F32)).astype(BF16)
    return _mm(m_bf16, hi) + (_mm(m_bf16, mid) + _mm(m_bf16, lo))


def _segsum(x, e2):
    out = []
    for i in range(x.shape[1] // PAIR):
        hi, lo = _bf16_split(x[:, i * PAIR:(i + 1) * PAIR])
        out.append(_mm(hi, e2) + _mm(lo, e2))
    return jnp.concatenate(out, axis=1)


def _full(shape):
    return pl.BlockSpec(shape, lambda *_: (0,) * len(shape))


def _ada_kernel(c_ref, w_ref, b_ref, o_ref):
    o_ref[0] = _dot3(_silu(c_ref[...]), *_bf16_split(w_ref[0])) + b_ref[0]


def _ada_call(cc, ada_w, ada_b):
    depth, d, d3 = ada_w.shape
    rows = cc.shape[0]
    tn = 1024
    return pl.pallas_call(
        _ada_kernel,
        grid=(depth, d3 // tn),
        in_specs=[pl.BlockSpec((rows, d), lambda l, j: (0, 0)),
                  pl.BlockSpec((1, d, tn), lambda l, j: (l, 0, j)),
                  pl.BlockSpec((1, 1, tn), lambda l, j: (l, 0, j))],
        out_specs=pl.BlockSpec((1, rows, tn), lambda l, j: (l, 0, j)),
        out_shape=jax.ShapeDtypeStruct((depth, rows, d3), F32),
        compiler_params=_cparams("arbitrary", "arbitrary"),
        name="ada_mod",
    )(cc, ada_w, ada_b.reshape(depth, 1, d3))


def _modulated_norm(x, nw, scale, shift):
    rs = lax.rsqrt(jnp.mean(x * x, axis=-1, keepdims=True) + NORM_EPS)
    return (x * rs * nw) * (1.0 + scale) + shift


def _ev_in_kernel(x_ref, sc_ref, sh_ref, nw_ref, w_ref, pa_ref, pr_ref, zb_ref):
    h = _modulated_norm(x_ref[0], nw_ref[...], sc_ref[0], sh_ref[0]).astype(BF16)
    pa_ref[0] = jnp.dot(h, w_ref[:, :A_COLS], preferred_element_type=F32).astype(BF16)
    pr_ref[0] = jnp.dot(h, w_ref[:, A_COLS:A_COLS + SHIFT_COLS], preferred_element_type=F32)
    zb_ref[0] = jnp.dot(h, w_ref[:, A_COLS + SHIFT_COLS:], preferred_element_type=F32).astype(BF16)


def _ev_in_call(x, scale, shift, nw, w_bf16, tm):
    nb, t, d = x.shape
    ncol = w_bf16.shape[1]
    row = lambda b, i: (b, i, 0)
    vec = lambda b, i: (b, 0, 0)
    return pl.pallas_call(
        _ev_in_kernel,
        grid=(nb, t // tm),
        in_specs=[pl.BlockSpec((1, tm, d), row),
                  pl.BlockSpec((1, 1, d), vec),
                  pl.BlockSpec((1, 1, d), vec),
                  _full((1, d)),
                  _full((d, ncol))],
        out_specs=[pl.BlockSpec((1, tm, A_COLS), row),
                   pl.BlockSpec((1, tm, SHIFT_COLS), row),
                   pl.BlockSpec((1, tm, RWKV_WIDTH), row)],
        out_shape=[jax.ShapeDtypeStruct((nb, t, A_COLS), BF16),
                   jax.ShapeDtypeStruct((nb, t, SHIFT_COLS), F32),
                   jax.ShapeDtypeStruct((nb, t, RWKV_WIDTH), BF16)],
        compiler_params=_cparams("parallel", "parallel"),
        name="ev_in",
    )(x, scale, shift, nw, w_bf16)


def _neighbours(cur, prev_halo, next_halo, is_first, is_last):
    n = cur.shape[0]
    rowi = lax.broadcasted_iota(jnp.int32, cur.shape, 0)
    prev_row = jnp.where(is_first, 0.0, prev_halo[prev_halo.shape[0] - 1:, :])
    next_row = jnp.where(is_last, 0.0, next_halo[0:1, :])
    prev = jnp.where(rowi == 0, prev_row, pltpu.roll(cur, 1, 0))
    nxt = jnp.where(rowi == n - 1, next_row, pltpu.roll(cur, n - 1, 0))
    return prev, nxt


OPERANDS = ("rh", "ah", "bc", "kc", "bt", "kt", "v", "bon")


def _scan_prep_pieces(reverse, cur_ref, prv_ref, nxt_ref, is_first, is_last, prm, w_dir, ops_ref, gc_ref, d):
    C = CHUNK
    W = RWKV_WIDTH
    mu_keep, mu_half, k_k, k_a, ka_keep, r_k, e2 = prm
    wa0, lora_hi, lora_lo = w_dir
    t = {}

    def shift():
        pr = cur_ref[0]
        prev, nxt = _neighbours(pr, prv_ref[0], nxt_ref[0], is_first, is_last)
        x = mu_keep * pr + mu_half * (prev + nxt)
        t.update(r=x[:, 0:W], k=x[:, W:2 * W], v=x[:, 2 * W:3 * W], wa=x[:, 3 * W:3 * W + 2 * LORA])

    def rates():
        lane = lax.broadcasted_iota(jnp.int32, (C, PAIR), 1)
        tw = jnp.where(lane < LORA, jnp.tanh(t["wa"]), t["wa"])
        lo = _dot3(tw, lora_hi, lora_lo)
        sig = 1.0 / (1.0 + jnp.exp(-(wa0 + lo)))
        t.update(lw=DECAY_SCALE * sig[:, :W], iclr=sig[:, W:])

    def keys():
        k, iclr = t["k"], t["iclr"]
        kkv = k * k_k
        kkn = kkv * lax.rsqrt(jnp.maximum(_segsum(kkv * kkv, e2), 1e-24))
        k_dir = k * (ka_keep + iclr * k_a)
        t.update(kkn=kkn, k_dir=k_dir, bvec=kkn * iclr)
        ops_ref[d, OPERANDS.index("bon")] = _segsum(t["r"] * k_dir * r_k, e2) * t["v"]

    def decays():
        lw = t["lw"]
        ti = lax.broadcasted_iota(jnp.int32, (C, C), 0)
        si = lax.broadcasted_iota(jnp.int32, (C, C), 1)
        tri = ((si >= ti) if reverse else (si <= ti)).astype(BF16)
        g_in = _exact_lhs_dot(tri, lw)
        g_tot = jnp.sum(lw, axis=0, keepdims=True)
        t.update(gam=jnp.exp(g_in), gam_ex=jnp.exp(g_in - lw), gam_inv=jnp.exp(-g_in), gam_c=jnp.exp(g_tot))

    def store():
        gam_c = t["gam_c"]
        bc = t["bvec"] * t["gam_inv"]
        kc = t["k_dir"] * t["gam_inv"]
        out = dict(rh=t["r"] * t["gam"], ah=-t["kkn"] * t["gam_ex"], bc=bc, kc=kc,
                   bt=bc * gam_c, kt=kc * gam_c, v=t["v"])
        for name, val in out.items():
            ops_ref[d, OPERANDS.index(name)] = val
        gc_ref[d] = gam_c

    return [shift, rates, keys, decays, store]


def _scan_kernel(n_chunks, has_s0, *refs):
    refs = list(refs)
    first_refs = (refs[0:3], refs[3:6])
    next_refs = (refs[6:9], refs[9:12])
    (muk_ref, muh_ref, kk_ref, ka_ref, kak_ref, rk_ref, wa0_ref, lh_ref, ll_ref, e_ref) = refs[12:22]
    pos = 22
    s0_ref = None
    if has_s0:
        s0_ref = refs[pos]
        pos += 1
    y_refs = refs[pos:pos + 2]
    bon_refs = refs[pos + 2:pos + 4]
    sfin_ref, st_ref, ops_a, gc_a, ops_b, gc_b = refs[pos + 4:pos + 10]

    C = CHUNK
    c = pl.program_id(1)
    prm = (muk_ref[...], muh_ref[...], kk_ref[...], ka_ref[...], kak_ref[...], rk_ref[...], e_ref[...])

    def pieces(d, src, is_first, is_last, ops_dst, gc_dst):
        cur, prv, nxt = src
        return _scan_prep_pieces(d == 1, cur, prv, nxt, is_first, is_last, prm,
                                 (wa0_ref[d], lh_ref[d], ll_ref[d]), ops_dst, gc_dst, d)

    @pl.when(c == 0)
    def _():
        if has_s0:
            st_ref[...] = s0_ref[0]
        else:
            st_ref[...] = jnp.zeros_like(st_ref)
        for piece in pieces(0, first_refs[0], True, n_chunks == 1, ops_a, gc_a) + \
                pieces(1, first_refs[1], n_chunks == 1, True, ops_a, gc_a):
            piece()

    nxt_f = jnp.minimum(c + 1, n_chunks - 1)
    nxt_b = jnp.maximum(n_chunks - 2 - c, 0)

    def chain(ops_ref, gc_ref, overlap):
        n2 = 2 * C
        ri = lax.broadcasted_iota(jnp.int32, (n2, n2), 0)
        ci = lax.broadcasted_iota(jnp.int32, (n2, n2), 1)
        same_head = (ri >= C) == (ci >= C)
        rt = ri & (C - 1)
        ct = ci & (C - 1)
        strict = (same_head & (ct < rt), same_head & (ct > rt))
        incl = (same_head & (ct <= rt), same_head & (ct >= rt))
        diag = ri == ci
        eye = diag.astype(F32)
        blocks = [same_head & ((rt >> s) == (ct >> s)) for s in range(1, C.bit_length())]
        first_head = lax.broadcasted_iota(jnp.int32, (C, PAIR), 1) < RWKV_HEAD

        def stk(z):
            return jnp.concatenate([jnp.where(first_head, z, 0.0), jnp.where(first_head, 0.0, z)], axis=0)

        def dup(z):
            return jnp.concatenate([z, z], axis=0)

        chains = [(d, slice(p * PAIR, (p + 1) * PAIR), p) for d in range(2) for p in range(N_PAIRS)]
        n_ch = len(chains)

        def operand(name, d, sl):
            return ops_ref[d, OPERANDS.index(name), :, sl]

        ah_s = [stk(operand("ah", d, sl)) for d, sl, _ in chains]
        rh_s = [stk(operand("rh", d, sl)) for d, sl, _ in chains]
        v_s = [stk(operand("v", d, sl)) for d, sl, _ in chains]
        pm = [_dot_nt(jnp.concatenate([ah_s[i], rh_s[i]], axis=0),
                      jnp.concatenate([dup(operand("bc", d, sl)), dup(operand("kc", d, sl))], axis=0))
              for i, (d, sl, _) in enumerate(chains)]
        overlap()
        a_ab = [jnp.where(strict[d], pm[i][:n2, :n2], 0.0) for i, (d, _, _) in enumerate(chains)]
        a_ak = [jnp.where(strict[d], pm[i][:n2, n2:], 0.0) for i, (d, _, _) in enumerate(chains)]
        a_r = [jnp.concatenate([jnp.where(incl[d], pm[i][n2:, :n2], 0.0),
                                jnp.where(incl[d], pm[i][n2:, n2:], 0.0)], axis=1)
               for i, (d, _, _) in enumerate(chains)]
        akv = [_dot(a_ak[i], v_s[i]) for i in range(n_ch)]
        tinv = [jnp.where(blocks[0], a, 0.0) + eye for a in a_ab]
        overlap()
        for lvl in range(1, len(blocks)):
            off = blocks[lvl] & ~blocks[lvl - 1]
            t1 = [_dot(jnp.where(off, a_ab[i], 0.0), tinv[i]) for i in range(n_ch)]
            overlap()
            t2 = [_dot(tinv[i], t1[i]) for i in range(n_ch)]
            tinv = [tinv[i] + t2[i] for i in range(n_ch)]
            overlap()
        xx = [_dot(tinv[i], jnp.concatenate([ah_s[i], akv[i]], axis=1)) for i in range(n_ch)]
        zeros = jnp.zeros((n2, PAIR), F32)
        rr = []
        for i, (d, sl, _) in enumerate(chains):
            rhs = jnp.concatenate([xx[i], jnp.concatenate([zeros, v_s[i]], axis=1)], axis=0)
            lhs = jnp.concatenate(
                [a_r[i], jnp.concatenate([stk(operand("bt", d, sl)).T, stk(operand("kt", d, sl)).T], axis=1)], axis=0)
            rr.append(_dot(lhs, rhs))
        for i, (d, sl, p) in enumerate(chains):
            q_t = rh_s[i] + rr[i][:n2, :PAIR]
            st = st_ref[d, p]
            st_hi, st_lo = _bf16_split(st)
            m_off = rr[i][n2:, :PAIR].astype(BF16)
            ys = _mm(q_t.astype(BF16), st_hi) + rr[i][:n2, PAIR:]
            g_col = jnp.sum(jnp.where(diag, gc_ref[d][:, sl], 0.0), axis=1, keepdims=True)
            st_ref[d, p] = (g_col * st + (_mm(m_off, st_hi) + _mm(m_off, st_lo))) + rr[i][n2:, PAIR:]
            y_refs[d][0, :, sl] = ys[:C] + ys[C:]
        for d in range(2):
            bon_refs[d][0] = ops_ref[d, OPERANDS.index("bon")]

    def step(ops_ref, gc_ref, ops_nxt, gc_nxt):
        prep_f = pieces(0, next_refs[0], nxt_f == 0, nxt_f == n_chunks - 1, ops_nxt, gc_nxt)
        prep_b = pieces(1, next_refs[1], nxt_b == 0, nxt_b == n_chunks - 1, ops_nxt, gc_nxt)
        background = [fn for pair in zip(prep_f, prep_b) for fn in pair]

        def overlap():
            if background:
                background.pop(0)()

        chain(ops_ref, gc_ref, overlap)
        while background:
            overlap()

    @pl.when(c % 2 == 0)
    def _():
        step(ops_a, gc_a, ops_b, gc_b)

    @pl.when(c % 2 == 1)
    def _():
        step(ops_b, gc_b, ops_a, gc_a)

    @pl.when(c == n_chunks - 1)
    def _():
        sfin_ref[0] = st_ref[...]


def _scan_call(pr, params, s0):
    nb, t, _ = pr.shape
    C = CHUNK
    n_chunks = t // C
    hb = C // HALO
    n_hblk = t // HALO

    def specs(cs_of):
        return [pl.BlockSpec((1, C, SHIFT_COLS), lambda b, c: (b, cs_of(c), 0)),
                pl.BlockSpec((1, HALO, SHIFT_COLS), lambda b, c: (b, jnp.maximum(cs_of(c) * hb - 1, 0), 0)),
                pl.BlockSpec((1, HALO, SHIFT_COLS), lambda b, c: (b, jnp.minimum((cs_of(c) + 1) * hb, n_hblk - 1), 0))]

    first_f = lambda c: 0
    first_b = lambda c: n_chunks - 1
    next_f = lambda c: jnp.minimum(c + 1, n_chunks - 1)
    next_b = lambda c: jnp.maximum(n_chunks - 2 - c, 0)
    st_spec = pl.BlockSpec((1, 2, N_PAIRS, PAIR, PAIR), lambda b, c: (b, 0, 0, 0, 0))
    in_specs = specs(first_f) + specs(first_b) + specs(next_f) + specs(next_b) + [_full(p.shape) for p in params]
    args = [pr] * 12 + list(params)
    if s0 is not None:
        in_specs.append(st_spec)
        args.append(s0)
    out_f = pl.BlockSpec((1, C, RWKV_WIDTH), lambda b, c: (b, c, 0))
    out_b = pl.BlockSpec((1, C, RWKV_WIDTH), lambda b, c: (b, n_chunks - 1 - c, 0))
    tok = jax.ShapeDtypeStruct((nb, t, RWKV_WIDTH), F32)
    return pl.pallas_call(
        functools.partial(_scan_kernel, n_chunks, s0 is not None),
        grid=(nb, n_chunks),
        in_specs=in_specs,
        out_specs=[out_f, out_b, out_f, out_b, st_spec],
        out_shape=[tok, tok, tok, tok, jax.ShapeDtypeStruct((nb, 2, N_PAIRS, PAIR, PAIR), F32)],
        scratch_shapes=[pltpu.VMEM((2, N_PAIRS, PAIR, PAIR), F32),
                        pltpu.VMEM((2, len(OPERANDS), C, RWKV_WIDTH), F32), pltpu.VMEM((2, 1, RWKV_WIDTH), F32),
                        pltpu.VMEM((2, len(OPERANDS), C, RWKV_WIDTH), F32), pltpu.VMEM((2, 1, RWKV_WIDTH), F32)],
        compiler_params=_cparams("parallel", "arbitrary"),
        name="rwkv_scan",
    )(*args)


def _ev_out_kernel(n_tiles, pa_ref, pap_ref, pan_ref, zb_ref, yf_ref, yb_ref, bf_ref, bb_ref,
                   cw_ref, lw_ref, lb_ref, e_ref, wo_ref, x_ref, g_ref, o_ref):
    i = pl.program_id(1)
    W = CONV_WIDTH
    pa = pa_ref[0].astype(F32)
    u, gate_b, gate_c, z = pa[:, 0:W], pa[:, W:2 * W], pa[:, 2 * W:3 * W], pa[:, 3 * W:4 * W]
    cu = gate_c * u
    pa_p = pap_ref[0].astype(F32)
    pa_n = pan_ref[0].astype(F32)
    cu_p = pa_p[:, 2 * W:3 * W] * pa_p[:, 0:W]
    cu_n = pa_n[:, 2 * W:3 * W] * pa_n[:, 0:W]
    prev, nxt = _neighbours(cu, cu_p, cu_n, i == 0, i == n_tiles - 1)
    cw = cw_ref[...]
    conv = cw[0:1] * prev + cw[1:2] * cu + cw[2:3] * nxt
    a_out = gate_b * conv * _silu(z)

    e = e_ref[...]
    ysum = yf_ref[0] + yb_ref[0]
    inv_n = 1.0 / RWKV_HEAD
    mean = _segsum(ysum, e) * inv_n
    dlt = ysum - mean
    var = _segsum(dlt * dlt, e) * inv_n
    rw = dlt * lax.rsqrt(var + RWKV_GN_EPS) * lw_ref[...] + lb_ref[...] + (bf_ref[0] + bb_ref[0])
    b_out = rw * _silu(zb_ref[0].astype(F32))

    y = (jnp.dot(a_out.astype(BF16), wo_ref[:W, :], preferred_element_type=F32)
         + jnp.dot(b_out.astype(BF16), wo_ref[W:, :], preferred_element_type=F32))
    o_ref[0] = x_ref[0] + g_ref[0] * y


def _ev_out_call(pa, zb, yf, yb, bf, bb, conv_w, lnx_w, lnx_b, e_bf16, wo_bf16, x, gate, tm):
    nb, t, d = x.shape
    n_tiles = t // tm
    hb = tm // HALO_BF16
    n_hblk = t // HALO_BF16
    row = lambda b, i: (b, i, 0)
    prv = lambda b, i: (b, jnp.maximum(i * hb - 1, 0), 0)
    nxt = lambda b, i: (b, jnp.minimum((i + 1) * hb, n_hblk - 1), 0)
    vec = lambda b, i: (b, 0, 0)
    wspec = pl.BlockSpec((1, tm, RWKV_WIDTH), row)
    return pl.pallas_call(
        functools.partial(_ev_out_kernel, n_tiles),
        grid=(nb, n_tiles),
        in_specs=[pl.BlockSpec((1, tm, A_COLS), row),
                  pl.BlockSpec((1, HALO_BF16, A_COLS), prv),
                  pl.BlockSpec((1, HALO_BF16, A_COLS), nxt),
                  wspec, wspec, wspec, wspec, wspec,
                  _full(conv_w.shape), _full(lnx_w.shape), _full(lnx_b.shape),
                  _full(e_bf16.shape), _full(wo_bf16.shape),
                  pl.BlockSpec((1, tm, d), row),
                  pl.BlockSpec((1, 1, d), vec)],
        out_specs=pl.BlockSpec((1, tm, d), row),
        out_shape=jax.ShapeDtypeStruct((nb, t, d), F32),
        compiler_params=_cparams("parallel", "parallel"),
        name="ev_out",
    )(pa, pa, pa, zb, yf, yb, bf, bb, conv_w, lnx_w, lnx_b, e_bf16, wo_bf16, x, gate)


def _head_rms(t, g):
    return t * lax.rsqrt(jnp.mean(t * t, axis=-1, keepdims=True) + NORM_EPS) * g


def _keys_values(p_kv, kvn, wkvb, gkn, gkr, gkr_sw, cos, sin, k_ref, v_ref):
    kv_a = p_kv[:, :KV_LORA]
    kr = p_kv[:, KV_LORA:KV_LORA + QK_ROPE]
    kr_sw = p_kv[:, KV_LORA + QK_ROPE:KV_LORA + 2 * QK_ROPE]
    rs = lax.rsqrt(jnp.mean(kr * kr, axis=-1, keepdims=True) + NORM_EPS)
    if cos is None:
        k_rope = kr * rs * gkr
    else:
        k_rope = ((kr * gkr) * cos + (kr_sw * gkr_sw) * sin) * rs
    k_rope = k_rope.astype(BF16)
    kv = jnp.dot(_head_rms(kv_a, kvn).astype(BF16), wkvb, preferred_element_type=F32)
    for h in range(MLA_HEADS):
        kn = _head_rms(kv[:, h * QK_NOPE:(h + 1) * QK_NOPE], gkn)
        k_ref[0, h, :, 0:QK_NOPE] = kn.astype(BF16)
        k_ref[0, h, :, QK_NOPE:QK_DIM] = k_rope
        v_ref[0, h, :, 0:V_HEAD] = kv[:, MLA_WIDTH + h * V_HEAD:MLA_WIDTH + (h + 1) * V_HEAD].astype(BF16)
        v_ref[0, h, :, V_HEAD:V_COLS] = jnp.ones((kv.shape[0], V_HEAD), BF16)


def _od_lat_kernel(x_ref, sc_ref, sh_ref, nw_ref, win_ref, qan_ref, kvn_ref, wqb_ref, wkvb_ref,
                   gqn_ref, gqr_ref, gqrs_ref, gkn_ref, gkr_ref, gkrs_ref, cos_ref, sin_ref,
                   kc_ref, vc_ref, q_ref, k_ref, v_ref, z_ref, *, n_ctx_tiles):
    i = pl.program_id(1)

    @pl.when(i < n_ctx_tiles)
    def _():
        k_ref[...] = kc_ref[...]
        v_ref[...] = vc_ref[...]

    @pl.when(i >= n_ctx_tiles)
    def _():
        _od_lat_tile(x_ref, sc_ref, sh_ref, nw_ref, win_ref, qan_ref, kvn_ref, wqb_ref, wkvb_ref,
                     gqn_ref, gqr_ref, gqrs_ref, gkn_ref, gkr_ref, gkrs_ref, cos_ref, sin_ref,
                     q_ref, k_ref, v_ref, z_ref)


def _od_lat_tile(x_ref, sc_ref, sh_ref, nw_ref, win_ref, qan_ref, kvn_ref, wqb_ref, wkvb_ref,
                 gqn_ref, gqr_ref, gqrs_ref, gkn_ref, gkr_ref, gkrs_ref, cos_ref, sin_ref,
                 q_ref, k_ref, v_ref, z_ref):
    h = _modulated_norm(x_ref[0], nw_ref[...], sc_ref[0], sh_ref[0]).astype(BF16)
    p = jnp.dot(h, win_ref[...], preferred_element_type=F32)
    kv_off = Q_LORA
    z_off = Q_LORA + KV_LORA + 2 * QK_ROPE
    z_ref[0] = p[:, z_off:].astype(BF16)
    cos = cos_ref[...]
    sin = sin_ref[...]
    _keys_values(p[:, kv_off:z_off], kvn_ref[...], wkvb_ref[...], gkn_ref[...], gkr_ref[...], gkrs_ref[...],
                 cos[:, :QK_ROPE], sin[:, :QK_ROPE], k_ref, v_ref)

    q = jnp.dot(_head_rms(p[:, :Q_LORA], qan_ref[...]).astype(BF16), wqb_ref[...],
                preferred_element_type=F32)
    rw = MLA_HEADS * QK_ROPE
    qr = q[:, MLA_WIDTH:MLA_WIDTH + rw]
    qr_sw = q[:, MLA_WIDTH + rw:MLA_WIDTH + 2 * rw]
    rot = (qr * gqr_ref[...]) * cos + (qr_sw * gqrs_ref[...]) * sin
    for hd in range(MLA_HEADS):
        qn = _head_rms(q[:, hd * QK_NOPE:(hd + 1) * QK_NOPE], gqn_ref[...]) * Q_SCALE
        q_ref[0, hd, :, 0:QK_NOPE] = qn.astype(BF16)
        sl = slice(hd * QK_ROPE, (hd + 1) * QK_ROPE)
        t = qr[:, sl]
        rs = lax.rsqrt(jnp.mean(t * t, axis=-1, keepdims=True) + NORM_EPS) * Q_SCALE
        q_ref[0, hd, :, QK_NOPE:QK_DIM] = (rot[:, sl] * rs).astype(BF16)


def _od_ctx_kernel(x_ref, sc_ref, sh_ref, nw_ref, win_ref, kvn_ref, wkvb_ref,
                   gkn_ref, gkr_ref, gkrs_ref, k_ref, v_ref):
    h = _modulated_norm(x_ref[0], nw_ref[...], sc_ref[0], sh_ref[0]).astype(BF16)
    p = jnp.dot(h, win_ref[...], preferred_element_type=F32)
    _keys_values(p, kvn_ref[...], wkvb_ref[...], gkn_ref[...], gkr_ref[...], gkrs_ref[...],
                 None, None, k_ref, v_ref)


def _od_ctx_call(ctx, scale, shift, nw, win_kv, kvn, wkvb, gkn, gkr, gkrs, tm):
    nb, tc, d = ctx.shape
    row = lambda b, i: (b, i, 0)
    vec = lambda b, i: (b, 0, 0)
    head = lambda b, i: (b, 0, i, 0)
    params = [nw, win_kv, kvn, wkvb, gkn, gkr, gkrs]
    return pl.pallas_call(
        _od_ctx_kernel,
        grid=(nb, tc // tm),
        in_specs=[pl.BlockSpec((1, tm, d), row), pl.BlockSpec((1, 1, d), vec), pl.BlockSpec((1, 1, d), vec)]
                 + [_full(p.shape) for p in params],
        out_specs=[pl.BlockSpec((1, MLA_HEADS, tm, QK_DIM), head),
                   pl.BlockSpec((1, MLA_HEADS, tm, V_COLS), head)],
        out_shape=[jax.ShapeDtypeStruct((nb, MLA_HEADS, tc, QK_DIM), BF16),
                   jax.ShapeDtypeStruct((nb, MLA_HEADS, tc, V_COLS), BF16)],
        compiler_params=_cparams("parallel", "parallel"),
        name="od_in_ctx",
    )(ctx, scale, shift, *params)


def _od_lat_call(x, scale, shift, params, cos, sin, k_ctx, v_ctx, tm):
    nb, t, d = x.shape
    tc = k_ctx.shape[2]
    assert tc % tm == 0
    n_off = tc // tm
    lat = lambda i: jnp.maximum(i - n_off, 0)
    cxt = lambda i: jnp.minimum(i, n_off - 1)
    row = lambda b, i: (b, lat(i), 0)
    vec = lambda b, i: (b, 0, 0)
    head_q = lambda b, i: (b, 0, lat(i), 0)
    head_c = lambda b, i: (b, 0, cxt(i), 0)
    head_k = lambda b, i: (b, 0, i, 0)
    tab = lambda b, i: (lat(i), 0)
    rw = MLA_HEADS * QK_ROPE
    return pl.pallas_call(
        functools.partial(_od_lat_kernel, n_ctx_tiles=n_off),
        grid=(nb, n_off + t // tm),
        in_specs=[pl.BlockSpec((1, tm, d), row), pl.BlockSpec((1, 1, d), vec), pl.BlockSpec((1, 1, d), vec)]
                 + [_full(p.shape) for p in params]
                 + [pl.BlockSpec((tm, rw), tab), pl.BlockSpec((tm, rw), tab),
                    pl.BlockSpec((1, MLA_HEADS, tm, QK_DIM), head_c),
                    pl.BlockSpec((1, MLA_HEADS, tm, V_COLS), head_c)],
        out_specs=[pl.BlockSpec((1, MLA_HEADS, tm, QK_DIM), head_q),
                   pl.BlockSpec((1, MLA_HEADS, tm, QK_DIM), head_k),
                   pl.BlockSpec((1, MLA_HEADS, tm, V_COLS), head_k),
                   pl.BlockSpec((1, tm, MLA_WIDTH), row)],
        out_shape=[jax.ShapeDtypeStruct((nb, MLA_HEADS, t, QK_DIM), BF16),
                   jax.ShapeDtypeStruct((nb, MLA_HEADS, tc + t, QK_DIM), BF16),
                   jax.ShapeDtypeStruct((nb, MLA_HEADS, tc + t, V_COLS), BF16),
                   jax.ShapeDtypeStruct((nb, t, MLA_WIDTH), BF16)],
        compiler_params=_cparams("parallel", "arbitrary"),
        name="od_in_lat",
    )(x, scale, shift, *params, cos, sin, k_ctx, v_ctx)


ATTN_SUB = 512


def _attn_kernel(q_ref, k_ref, v_ref, o_ref):
    k = k_ref[0, 0]
    v = v_ref[0, 0]
    sub = min(ATTN_SUB, q_ref.shape[2])
    n_sub = q_ref.shape[2] // sub

    def scores(i):
        return lax.dot_general(q_ref[0, 0, i * sub:(i + 1) * sub, :], k, (((1,), (1,)), ((), ())),
                               preferred_element_type=F32)

    s = scores(0)
    for i in range(n_sub):
        s_next = scores(i + 1) if i + 1 < n_sub else None
        p = jnp.exp2((s - jnp.max(s, axis=-1, keepdims=True)).astype(BF16))
        ov = jnp.dot(p, v, preferred_element_type=F32)
        o_ref[0, i * sub:(i + 1) * sub, :] = (ov[:, :V_HEAD] / ov[:, V_HEAD:]).astype(BF16)
        s = s_next


def _attn_call(q, k, v, tq):
    nb, nh, t, _ = q.shape
    tk = k.shape[2]
    whole = lambda b, h, i: (b, h, 0, 0)
    return pl.pallas_call(
        _attn_kernel,
        grid=(nb, nh, t // tq),
        in_specs=[pl.BlockSpec((1, 1, tq, QK_DIM), lambda b, h, i: (b, h, i, 0)),
                  pl.BlockSpec((1, 1, tk, QK_DIM), whole),
                  pl.BlockSpec((1, 1, tk, V_COLS), whole)],
        out_specs=pl.BlockSpec((1, tq, V_HEAD), lambda b, h, i: (b, i, h)),
        out_shape=jax.ShapeDtypeStruct((nb, t, nh * V_HEAD), BF16),
        compiler_params=_cparams("parallel", "parallel", "parallel"),
        name="mla_attn",
    )(q, k, v)


def _od_out_kernel(o_ref, z_ref, wo_ref, x_ref, g_ref, out_ref):
    a = (o_ref[0].astype(F32) * _silu(z_ref[0].astype(F32))).astype(BF16)
    out_ref[0] = x_ref[0] + g_ref[0] * jnp.dot(a, wo_ref[...], preferred_element_type=F32)


def _od_out_call(o, z, wo_bf16, x, gate, tm):
    nb, t, d = x.shape
    row = lambda b, i: (b, i, 0)
    return pl.pallas_call(
        _od_out_kernel,
        grid=(nb, t // tm),
        in_specs=[pl.BlockSpec((1, tm, MLA_WIDTH), row), pl.BlockSpec((1, tm, MLA_WIDTH), row),
                  _full(wo_bf16.shape), pl.BlockSpec((1, tm, d), row),
                  pl.BlockSpec((1, 1, d), lambda b, i: (b, 0, 0))],
        out_specs=pl.BlockSpec((1, tm, d), row),
        out_shape=jax.ShapeDtypeStruct((nb, t, d), F32),
        compiler_params=_cparams("parallel", "parallel"),
        name="od_out",
    )(o, z, wo_bf16, x, gate)


def _rope_tables(n):
    rows = n // GRID_W
    row = jnp.repeat(jnp.arange(rows, dtype=F32), GRID_W)
    col = jnp.tile(jnp.arange(GRID_W, dtype=F32), rows)
    inv_freq = ROPE_THETA ** (-jnp.arange(ROPE_PAIRS, dtype=F32) / ROPE_PAIRS)
    ar = row[:, None] * inv_freq
    ac = col[:, None] * inv_freq
    cos = jnp.concatenate([jnp.cos(ar), jnp.cos(ar), jnp.cos(ac), jnp.cos(ac)], axis=1)
    sin = jnp.concatenate([-jnp.sin(ar), jnp.sin(ar), -jnp.sin(ac), jnp.sin(ac)], axis=1)
    return cos, sin


def _row_tile(t, want):
    tm = min(t, want)
    assert t % tm == 0 and tm % HALO == 0
    return tm


def kernel(x, c, ctx, c_ctx, ada_w, ada_b, norm_w, ev_w_in, ev_conv_w, ev_mu, ev_k_k, ev_k_a, ev_w0, ev_w2, ev_a0, ev_a2, ev_r_k, ev_lnx_w, ev_lnx_b, ev_w_out, od_w_in, od_q_a_norm, od_kv_a_norm, od_w_qb, od_w_kvb, od_gq_nope, od_gq_rope, od_gk_nope, od_gk_rope, od_w_o):
    nb, t, d = x.shape
    tc = ctx.shape[1]
    assert t % CHUNK == 0 and tc % CHUNK == 0 and t % GRID_W == 0

    rows = -(-(nb + 1) // HALO) * HALO
    cc = jnp.zeros((rows, d), F32).at[:nb].set(c).at[nb].set(c_ctx)
    mod = _ada_call(cc, ada_w, ada_b)

    def mods(layer):
        m = mod[layer]
        lat = [m[:nb, i * d:(i + 1) * d].reshape(nb, 1, d) for i in range(3)]
        cx = [jnp.broadcast_to(m[nb, i * d:(i + 1) * d].reshape(1, 1, d), (nb, 1, d)) for i in range(3)]
        return lat, cx

    e_np = np.arange(PAIR)[:, None] // RWKV_HEAD == np.arange(PAIR)[None, :] // RWKV_HEAD
    e_bf16 = jnp.asarray(e_np, BF16)

    (sh_l, sc_l, g_l), (sh_c, sc_c, g_c) = mods(0)
    nw0 = norm_w[0].reshape(1, d)
    w_in0 = ev_w_in[0].astype(BF16)
    tm_l = _row_tile(t, 256)
    tm_c = _row_tile(tc, 256)
    pa_c, pr_c, zb_c = _ev_in_call(ctx, sc_c, sh_c, nw0, w_in0, tm_c)
    pa_l, pr_l, zb_l = _ev_in_call(x, sc_l, sh_l, nw0, w_in0, tm_l)

    W = RWKV_WIDTH
    vec = lambda a: a.reshape(1, -1)
    lora = jnp.zeros((2, 2 * LORA, 2 * W), F32)
    lora = lora.at[:, :LORA, :W].set(ev_w2[0]).at[:, LORA:, W:].set(ev_a2[0])
    lora_hi = lora.astype(BF16)
    lora_lo = (lora - lora_hi.astype(F32)).astype(BF16)
    wa0 = jnp.concatenate([ev_w0[0], ev_a0[0]], axis=-1).reshape(2, 1, 2 * W)
    params = [vec(1.0 - ev_mu[0]), vec(0.5 * ev_mu[0]), vec(ev_k_k[0]), vec(ev_k_a[0]), vec(1.0 - ev_k_a[0]),
              vec(ev_r_k[0]), wa0, lora_hi, lora_lo, e_bf16]
    yf_c, yb_c, bf_c, bb_c, s_c = _scan_call(pr_c, params, None)
    yf_l, yb_l, bf_l, bb_l, _ = _scan_call(pr_l, params, s_c)

    wo0 = ev_w_out[0].astype(BF16)
    lw, lb = vec(ev_lnx_w[0]), vec(ev_lnx_b[0])
    x1 = _ev_out_call(pa_l, zb_l, yf_l, yb_l, bf_l, bb_l, ev_conv_w[0], lw, lb, e_bf16, wo0, x, g_l, tm_l)
    ctx1 = _ev_out_call(pa_c, zb_c, yf_c, yb_c, bf_c, bb_c, ev_conv_w[0], lw, lb, e_bf16, wo0, ctx, g_c, tm_c)

    (sh_l, sc_l, g_l), (sh_c, sc_c, _) = mods(1)
    nw1 = norm_w[1].reshape(1, d)
    swap = np.arange(QK_ROPE) ^ ROPE_PAIRS
    hq = np.arange(MLA_HEADS)[:, None] * QK_DIM
    idx_q = np.concatenate([(hq + np.arange(QK_NOPE)).ravel(),
                            (hq + QK_NOPE + np.arange(QK_ROPE)).ravel(),
                            (hq + QK_NOPE + swap).ravel()])
    hk = np.arange(MLA_HEADS)[:, None] * (QK_NOPE + V_HEAD)
    idx_kv = np.concatenate([(hk + np.arange(QK_NOPE)).ravel(), (hk + QK_NOPE + np.arange(V_HEAD)).ravel()])
    r_off = Q_LORA + KV_LORA
    idx_in = np.concatenate([np.arange(r_off + QK_ROPE), r_off + swap,
                             np.arange(r_off + QK_ROPE, od_w_in.shape[2])])
    w_in1 = od_w_in[0][:, idx_in].astype(BF16)
    wqb = od_w_qb[0][:, idx_q].astype(BF16)
    wkvb = od_w_kvb[0][:, idx_kv].astype(BF16)
    gqr = jnp.tile(od_gq_rope[0], MLA_HEADS).reshape(1, -1)
    gqrs = jnp.tile(od_gq_rope[0][swap], MLA_HEADS).reshape(1, -1)
    gkr, gkrs = vec(od_gk_rope[0]), vec(od_gk_rope[0][swap])
    gqn, gkn = vec(od_gq_nope[0]), vec(od_gk_nope[0])
    qan, kvn = vec(od_q_a_norm[0]), vec(od_kv_a_norm[0])
    cos, sin = _rope_tables(t)
    cos = jnp.tile(cos, (1, MLA_HEADS))
    sin = jnp.tile(sin, (1, MLA_HEADS))

    k_c, v_c = _od_ctx_call(ctx1, sc_c, sh_c, nw1, w_in1[:, Q_LORA:Q_LORA + KV_LORA + 2 * QK_ROPE],
                            kvn, wkvb, gkn, gkr, gkrs, _row_tile(tc, 256))
    params = [nw1, w_in1, qan, kvn, wqb, wkvb, gqn, gqr, gqrs, gkn, gkr, gkrs]
    q, k_all, v_all, z = _od_lat_call(x1, sc_l, sh_l, params, cos, sin, k_c, v_c, _row_tile(tc, 256))
    o = _attn_call(q, k_all, v_all, _row_tile(t, 2 * ATTN_SUB))
    return _od_out_call(o, z, od_w_o[0].astype(BF16), x1, g_l, _row_tile(t, 256))
```

```python
import functools

import numpy as np
import jax
import jax.numpy as jnp
from jax import lax
from jax.experimental import pallas as pl
from jax.experimental.pallas import tpu as pltpu

F32 = jnp.float32
BF16 = jnp.bfloat16

NORM_EPS = 1e-6
GRID_W = 64

CONV_WIDTH = 512
RWKV_HEAD = 64
RWKV_WIDTH = 512
LORA = 64
RWKV_GN_EPS = 64e-5
DECAY_SCALE = -float(np.exp(-0.5))
A_COLS = 4 * CONV_WIDTH
SHIFT_COLS = 3 * RWKV_WIDTH + 2 * LORA
CHUNK = 64
PAIR = 2 * RWKV_HEAD
N_PAIRS = RWKV_WIDTH // PAIR

MLA_HEADS = 8
QK_NOPE = 128
QK_ROPE = 64
QK_DIM = QK_NOPE + QK_ROPE
V_HEAD = 128
Q_LORA = 384
KV_LORA = 256
MLA_WIDTH = MLA_HEADS * V_HEAD
SM_SCALE = QK_DIM ** -0.5
LOG2E = 1.4426950408889634
Q_SCALE = SM_SCALE * LOG2E
V_COLS = 2 * V_HEAD
ROPE_THETA = 10000.0
ROPE_PAIRS = QK_ROPE // 4
LANES = 128

VMEM_LIMIT_BYTES = 56 * 1024 * 1024
HALO = 8
HALO_BF16 = 16


def _cparams(*sem):
    return pltpu.CompilerParams(dimension_semantics=sem, vmem_limit_bytes=VMEM_LIMIT_BYTES)


def _silu(x):
    return x / (1.0 + jnp.exp(-x))


def _dot(a, b):
    return jnp.dot(a.astype(BF16), b.astype(BF16), preferred_element_type=F32)


def _dot_nt(a, b):
    return lax.dot_general(a.astype(BF16), b.astype(BF16), (((1,), (1,)), ((), ())),
                           preferred_element_type=F32)


def _bf16_split(x):
    hi = x.astype(BF16)
    return hi, (x - hi.astype(F32)).astype(BF16)


def _mm(a, b):
    return jnp.dot(a, b, preferred_element_type=F32)


def _dot3(a, b_hi, b_lo):
    a_hi, a_lo = _bf16_split(a)
    return _mm(a_hi, b_hi) + (_mm(a_lo, b_hi) + _mm(a_hi, b_lo))


def _exact_lhs_dot(m_bf16, x):
    hi = x.astype(BF16)
    r1 = x - hi.astype(F32)
    mid = r1.astype(BF16)
    lo = (r1 - mid.astype(F32)).astype(BF16)
    return _mm(m_bf16, hi) + (_mm(m_bf16, mid) + _mm(m_bf16, lo))


def _segsum(x, e2):
    out = []
    for i in range(x.shape[1] // PAIR):
        hi, lo = _bf16_split(x[:, i * PAIR:(i + 1) * PAIR])
        out.append(_mm(hi, e2) + _mm(lo, e2))
    return jnp.concatenate(out, axis=1)


def _full(shape):
    return pl.BlockSpec(shape, lambda *_: (0,) * len(shape))


def _ada_kernel(c_ref, w_ref, b_ref, o_ref):
    o_ref[0] = _dot3(_silu(c_ref[...]), *_bf16_split(w_ref[0])) + b_ref[0]


def _ada_call(cc, ada_w, ada_b):
    depth, d, d3 = ada_w.shape
    rows = cc.shape[0]
    tn = 1024
    return pl.pallas_call(
        _ada_kernel,
        grid=(depth, d3 // tn),
        in_specs=[pl.BlockSpec((rows, d), lambda l, j: (0, 0)),
                  pl.BlockSpec((1, d, tn), lambda l, j: (l, 0, j)),
                  pl.BlockSpec((1, 1, tn), lambda l, j: (l, 0, j))],
        out_specs=pl.BlockSpec((1, rows, tn), lambda l, j: (l, 0, j)),
        out_shape=jax.ShapeDtypeStruct((depth, rows, d3), F32),
        compiler_params=_cparams("arbitrary", "arbitrary"),
        name="ada_mod",
    )(cc, ada_w, ada_b.reshape(depth, 1, d3))


def _modulated_norm(x, nw, scale, shift):
    rs = lax.rsqrt(jnp.mean(x * x, axis=-1, keepdims=True) + NORM_EPS)
    return (x * rs * nw) * (1.0 + scale) + shift


def _ev_in_kernel(x_ref, sc_ref, sh_ref, nw_ref, w_ref, pa_ref, pr_ref, zb_ref):
    h = _modulated_norm(x_ref[0], nw_ref[...], sc_ref[0], sh_ref[0]).astype(BF16)
    pa_ref[0] = jnp.dot(h, w_ref[:, :A_COLS], preferred_element_type=F32).astype(BF16)
    pr_ref[0] = jnp.dot(h, w_ref[:, A_COLS:A_COLS + SHIFT_COLS], preferred_element_type=F32)
    zb_ref[0] = jnp.dot(h, w_ref[:, A_COLS + SHIFT_COLS:], preferred_element_type=F32).astype(BF16)


def _ev_in_call(x, scale, shift, nw, w_bf16, tm):
    nb, t, d = x.shape
    ncol = w_bf16.shape[1]
    row = lambda b, i: (b, i, 0)
    vec = lambda b, i: (b, 0, 0)
    return pl.pallas_call(
        _ev_in_kernel,
        grid=(nb, t // tm),
        in_specs=[pl.BlockSpec((1, tm, d), row),
                  pl.BlockSpec((1, 1, d), vec),
                  pl.BlockSpec((1, 1, d), vec),
                  _full((1, d)),
                  _full((d, ncol))],
        out_specs=[pl.BlockSpec((1, tm, A_COLS), row),
                   pl.BlockSpec((1, tm, SHIFT_COLS), row),
                   pl.BlockSpec((1, tm, RWKV_WIDTH), row)],
        out_shape=[jax.ShapeDtypeStruct((nb, t, A_COLS), BF16),
                   jax.ShapeDtypeStruct((nb, t, SHIFT_COLS), F32),
                   jax.ShapeDtypeStruct((nb, t, RWKV_WIDTH), BF16)],
        compiler_params=_cparams("parallel", "parallel"),
        name="ev_in",
    )(x, scale, shift, nw, w_bf16)


def _neighbours(cur, prev_halo, next_halo, is_first, is_last):
    n = cur.shape[0]
    rowi = lax.broadcasted_iota(jnp.int32, cur.shape, 0)
    prev_row = jnp.where(is_first, 0.0, prev_halo[prev_halo.shape[0] - 1:, :])
    next_row = jnp.where(is_last, 0.0, next_halo[0:1, :])
    prev = jnp.where(rowi == 0, prev_row, pltpu.roll(cur, 1, 0))
    nxt = jnp.where(rowi == n - 1, next_row, pltpu.roll(cur, n - 1, 0))
    return prev, nxt


OPERANDS = ("rh", "ah", "bc", "kc", "bt", "kt", "v", "bon")


def _scan_prep_pieces(reverse, cur_ref, prv_ref, nxt_ref, is_first, is_last, prm, w_dir, ops_ref, gc_ref, d):
    C = CHUNK
    W = RWKV_WIDTH
    mu_keep, mu_half, k_k, k_a, ka_keep, r_k, e2 = prm
    wa0, lora_hi, lora_lo = w_dir
    t = {}

    def shift():
        pr = cur_ref[0]
        prev, nxt = _neighbours(pr, prv_ref[0], nxt_ref[0], is_first, is_last)
        x = mu_keep * pr + mu_half * (prev + nxt)
        t.update(r=x[:, 0:W], k=x[:, W:2 * W], v=x[:, 2 * W:3 * W], wa=x[:, 3 * W:3 * W + 2 * LORA])

    def rates():
        lane = lax.broadcasted_iota(jnp.int32, (C, PAIR), 1)
        tw = jnp.where(lane < LORA, jnp.tanh(t["wa"]), t["wa"])
        lo = _dot3(tw, lora_hi, lora_lo)
        sig = 1.0 / (1.0 + jnp.exp(-(wa0 + lo)))
        t.update(lw=DECAY_SCALE * sig[:, :W], iclr=sig[:, W:])

    def keys():
        k, iclr = t["k"], t["iclr"]
        kkv = k * k_k
        kkn = kkv * lax.rsqrt(jnp.maximum(_segsum(kkv * kkv, e2), 1e-24))
        k_dir = k * (ka_keep + iclr * k_a)
        t.update(kkn=kkn, k_dir=k_dir, bvec=kkn * iclr)
        ops_ref[d, OPERANDS.index("bon")] = _segsum(t["r"] * k_dir * r_k, e2) * t["v"]

    def decays():
        lw = t["lw"]
        ti = lax.broadcasted_iota(jnp.int32, (C, C), 0)
        si = lax.broadcasted_iota(jnp.int32, (C, C), 1)
        tri = ((si >= ti) if reverse else (si <= ti)).astype(BF16)
        g_in = _exact_lhs_dot(tri, lw)
        g_tot = jnp.sum(lw, axis=0, keepdims=True)
        t.update(gam=jnp.exp(g_in), gam_ex=jnp.exp(g_in - lw), gam_inv=jnp.exp(-g_in), gam_c=jnp.exp(g_tot))

    def store():
        gam_c = t["gam_c"]
        bc = t["bvec"] * t["gam_inv"]
        kc = t["k_dir"] * t["gam_inv"]
        out = dict(rh=t["r"] * t["gam"], ah=-t["kkn"] * t["gam_ex"], bc=bc, kc=kc,
                   bt=bc * gam_c, kt=kc * gam_c, v=t["v"])
        for name, val in out.items():
            ops_ref[d, OPERANDS.index(name)] = val
        gc_ref[d] = gam_c

    return [shift, rates, keys, decays, store]


def _scan_kernel(n_chunks, has_s0, *refs):
    refs = list(refs)
    first_refs = (refs[0:3], refs[3:6])
    next_refs = (refs[6:9], refs[9:12])
    (muk_ref, muh_ref, kk_ref, ka_ref, kak_ref, rk_ref, wa0_ref, lh_ref, ll_ref, e_ref) = refs[12:22]
    pos = 22
    s0_ref = None
    if has_s0:
        s0_ref = refs[pos]
        pos += 1
    y_refs = refs[pos:pos + 2]
    bon_refs = refs[pos + 2:pos + 4]
    sfin_ref, st_ref, ops_a, gc_a, ops_b, gc_b = refs[pos + 4:pos + 10]

    C = CHUNK
    c = pl.program_id(1)
    prm = (muk_ref[...], muh_ref[...], kk_ref[...], ka_ref[...], kak_ref[...], rk_ref[...], e_ref[...])

    def pieces(d, src, is_first, is_last, ops_dst, gc_dst):
        cur, prv, nxt = src
        return _scan_prep_pieces(d == 1, cur, prv, nxt, is_first, is_last, prm,
                                 (wa0_ref[d], lh_ref[d], ll_ref[d]), ops_dst, gc_dst, d)

    @pl.when(c == 0)
    def _():
        if has_s0:
            st_ref[...] = s0_ref[0]
        else:
            st_ref[...] = jnp.zeros_like(st_ref)
        for piece in pieces(0, first_refs[0], True, n_chunks == 1, ops_a, gc_a) + \
                pieces(1, first_refs[1], n_chunks == 1, True, ops_a, gc_a):
            piece()

    nxt_f = jnp.minimum(c + 1, n_chunks - 1)
    nxt_b = jnp.maximum(n_chunks - 2 - c, 0)

    def chain(ops_ref, gc_ref, overlap):
        n2 = 2 * C
        ri = lax.broadcasted_iota(jnp.int32, (n2, n2), 0)
        ci = lax.broadcasted_iota(jnp.int32, (n2, n2), 1)
        same_head = (ri >= C) == (ci >= C)
        rt = ri & (C - 1)
        ct = ci & (C - 1)
        strict = (same_head & (ct < rt), same_head & (ct > rt))
        incl = (same_head & (ct <= rt), same_head & (ct >= rt))
        diag = ri == ci
        eye = diag.astype(F32)
        blocks = [same_head & ((rt >> s) == (ct >> s)) for s in range(1, C.bit_length())]
        first_head = lax.broadcasted_iota(jnp.int32, (C, PAIR), 1) < RWKV_HEAD

        def stk(z):
            return jnp.concatenate([jnp.where(first_head, z, 0.0), jnp.where(first_head, 0.0, z)], axis=0)

        def dup(z):
            return jnp.concatenate([z, z], axis=0)

        chains = [(d, slice(p * PAIR, (p + 1) * PAIR), p) for d in range(2) for p in range(N_PAIRS)]
        n_ch = len(chains)

        def operand(name, d, sl):
            return ops_ref[d, OPERANDS.index(name), :, sl]

        ah_s = [stk(operand("ah", d, sl)) for d, sl, _ in chains]
        rh_s = [stk(operand("rh", d, sl)) for d, sl, _ in chains]
        v_s = [stk(operand("v", d, sl)) for d, sl, _ in chains]
        pm = [_dot_nt(jnp.concatenate([ah_s[i], rh_s[i]], axis=0),
                      jnp.concatenate([dup(operand("bc", d, sl)), dup(operand("kc", d, sl))], axis=0))
              for i, (d, sl, _) in enumerate(chains)]
        overlap()
        a_ab = [jnp.where(strict[d], pm[i][:n2, :n2], 0.0) for i, (d, _, _) in enumerate(chains)]
        a_ak = [jnp.where(strict[d], pm[i][:n2, n2:], 0.0) for i, (d, _, _) in enumerate(chains)]
        a_r = [jnp.concatenate([jnp.where(incl[d], pm[i][n2:, :n2], 0.0),
                                jnp.where(incl[d], pm[i][n2:, n2:], 0.0)], axis=1)
               for i, (d, _, _) in enumerate(chains)]
        akv = [_dot(a_ak[i], v_s[i]) for i in range(n_ch)]
        tinv = [jnp.where(blocks[0], a, 0.0) + eye for a in a_ab]
        overlap()
        for lvl in range(1, len(blocks)):
            off = blocks[lvl] & ~blocks[lvl - 1]
            t1 = [_dot(jnp.where(off, a_ab[i], 0.0), tinv[i]) for i in range(n_ch)]
            overlap()
            t2 = [_dot(tinv[i], t1[i]) for i in range(n_ch)]
            tinv = [tinv[i] + t2[i] for i in range(n_ch)]
            overlap()
        xx = [_dot(tinv[i], jnp.concatenate([ah_s[i], akv[i]], axis=1)) for i in range(n_ch)]
        zeros = jnp.zeros((n2, PAIR), F32)
        rr = []
        for i, (d, sl, _) in enumerate(chains):
            rhs = jnp.concatenate([xx[i], jnp.concatenate([zeros, v_s[i]], axis=1)], axis=0)
            lhs = jnp.concatenate(
                [a_r[i], jnp.concatenate([stk(operand("bt", d, sl)).T, stk(operand("kt", d, sl)).T], axis=1)], axis=0)
            rr.append(_dot(lhs, rhs))
        for i, (d, sl, p) in enumerate(chains):
            q_t = rh_s[i] + rr[i][:n2, :PAIR]
            st = st_ref[d, p]
            st_hi, st_lo = _bf16_split(st)
            m_off = rr[i][n2:, :PAIR].astype(BF16)
            ys = _mm(q_t.astype(BF16), st_hi) + rr[i][:n2, PAIR:]
            g_col = jnp.sum(jnp.where(diag, gc_ref[d][:, sl], 0.0), axis=1, keepdims=True)
            st_ref[d, p] = (g_col * st + (_mm(m_off, st_hi) + _mm(m_off, st_lo))) + rr[i][n2:, PAIR:]
            y_refs[d][0, :, sl] = ys[:C] + ys[C:]
        for d in range(2):
            bon_refs[d][0] = ops_ref[d, OPERANDS.index("bon")]

    def step(ops_ref, gc_ref, ops_nxt, gc_nxt):
        prep_f = pieces(0, next_refs[0], nxt_f == 0, nxt_f == n_chunks - 1, ops_nxt, gc_nxt)
        prep_b = pieces(1, next_refs[1], nxt_b == 0, nxt_b == n_chunks - 1, ops_nxt, gc_nxt)
        background = [fn for pair in zip(prep_f, prep_b) for fn in pair]

        def overlap():
            if background:
                background.pop(0)()

        chain(ops_ref, gc_ref, overlap)
        while background:
            overlap()

    @pl.when(c % 2 == 0)
    def _():
        step(ops_a, gc_a, ops_b, gc_b)

    @pl.when(c % 2 == 1)
    def _():
        step(ops_b, gc_b, ops_a, gc_a)

    @pl.when(c == n_chunks - 1)
    def _():
        sfin_ref[0] = st_ref[...]


def _scan_call(pr, params, s0):
    nb, t, _ = pr.shape
    C = CHUNK
    n_chunks = t // C
    hb = C // HALO
    n_hblk = t // HALO

    def specs(cs_of):
        return [pl.BlockSpec((1, C, SHIFT_COLS), lambda b, c: (b, cs_of(c), 0)),
                pl.BlockSpec((1, HALO, SHIFT_COLS), lambda b, c: (b, jnp.maximum(cs_of(c) * hb - 1, 0), 0)),
                pl.BlockSpec((1, HALO, SHIFT_COLS), lambda b, c: (b, jnp.minimum((cs_of(c) + 1) * hb, n_hblk - 1), 0))]

    first_f = lambda c: 0
    first_b = lambda c: n_chunks - 1
    next_f = lambda c: jnp.minimum(c + 1, n_chunks - 1)
    next_b = lambda c: jnp.maximum(n_chunks - 2 - c, 0)
    st_spec = pl.BlockSpec((1, 2, N_PAIRS, PAIR, PAIR), lambda b, c: (b, 0, 0, 0, 0))
    in_specs = specs(first_f) + specs(first_b) + specs(next_f) + specs(next_b) + [_full(p.shape) for p in params]
    args = [pr] * 12 + list(params)
    if s0 is not None:
        in_specs.append(st_spec)
        args.append(s0)
    out_f = pl.BlockSpec((1, C, RWKV_WIDTH), lambda b, c: (b, c, 0))
    out_b = pl.BlockSpec((1, C, RWKV_WIDTH), lambda b, c: (b, n_chunks - 1 - c, 0))
    tok = jax.ShapeDtypeStruct((nb, t, RWKV_WIDTH), F32)
    return pl.pallas_call(
        functools.partial(_scan_kernel, n_chunks, s0 is not None),
        grid=(nb, n_chunks),
        in_specs=in_specs,
        out_specs=[out_f, out_b, out_f, out_b, st_spec],
        out_shape=[tok, tok, tok, tok, jax.ShapeDtypeStruct((nb, 2, N_PAIRS, PAIR, PAIR), F32)],
        scratch_shapes=[pltpu.VMEM((2, N_PAIRS, PAIR, PAIR), F32),
                        pltpu.VMEM((2, len(OPERANDS), C, RWKV_WIDTH), F32), pltpu.VMEM((2, 1, RWKV_WIDTH), F32),
                        pltpu.VMEM((2, len(OPERANDS), C, RWKV_WIDTH), F32), pltpu.VMEM((2, 1, RWKV_WIDTH), F32)],
        compiler_params=_cparams("parallel", "arbitrary"),
        name="rwkv_scan",
    )(*args)


def _ev_out_kernel(n_tiles, pa_ref, pap_ref, pan_ref, zb_ref, yf_ref, yb_ref, bf_ref, bb_ref,
                   cw_ref, lw_ref, lb_ref, e_ref, wo_ref, x_ref, g_ref, o_ref):
    i = pl.program_id(1)
    W = CONV_WIDTH
    pa = pa_ref[0].astype(F32)
    u, gate_b, gate_c, z = pa[:, 0:W], pa[:, W:2 * W], pa[:, 2 * W:3 * W], pa[:, 3 * W:4 * W]
    cu = gate_c * u
    pa_p = pap_ref[0].astype(F32)
    pa_n = pan_ref[0].astype(F32)
    cu_p = pa_p[:, 2 * W:3 * W] * pa_p[:, 0:W]
    cu_n = pa_n[:, 2 * W:3 * W] * pa_n[:, 0:W]
    prev, nxt = _neighbours(cu, cu_p, cu_n, i == 0, i == n_tiles - 1)
    cw = cw_ref[...]
    conv = cw[0:1] * prev + cw[1:2] * cu + cw[2:3] * nxt
    a_out = gate_b * conv * _silu(z)

    e = e_ref[...]
    ysum = yf_ref[0] + yb_ref[0]
    inv_n = 1.0 / RWKV_HEAD
    mean = _segsum(ysum, e) * inv_n
    dlt = ysum - mean
    var = _segsum(dlt * dlt, e) * inv_n
    rw = dlt * lax.rsqrt(var + RWKV_GN_EPS) * lw_ref[...] + lb_ref[...] + (bf_ref[0] + bb_ref[0])
    b_out = rw * _silu(zb_ref[0].astype(F32))

    y = (jnp.dot(a_out.astype(BF16), wo_ref[:W, :], preferred_element_type=F32)
         + jnp.dot(b_out.astype(BF16), wo_ref[W:, :], preferred_element_type=F32))
    o_ref[0] = x_ref[0] + g_ref[0] * y


def _ev_out_call(pa, zb, yf, yb, bf, bb, conv_w, lnx_w, lnx_b, e_bf16, wo_bf16, x, gate, tm):
    nb, t, d = x.shape
    n_tiles = t // tm
    hb = tm // HALO_BF16
    n_hblk = t // HALO_BF16
    row = lambda b, i: (b, i, 0)
    prv = lambda b, i: (b, jnp.maximum(i * hb - 1, 0), 0)
    nxt = lambda b, i: (b, jnp.minimum((i + 1) * hb, n_hblk - 1), 0)
    vec = lambda b, i: (b, 0, 0)
    wspec = pl.BlockSpec((1, tm, RWKV_WIDTH), row)
    return pl.pallas_call(
        functools.partial(_ev_out_kernel, n_tiles),
        grid=(nb, n_tiles),
        in_specs=[pl.BlockSpec((1, tm, A_COLS), row),
                  pl.BlockSpec((1, HALO_BF16, A_COLS), prv),
                  pl.BlockSpec((1, HALO_BF16, A_COLS), nxt),
                  wspec, wspec, wspec, wspec, wspec,
                  _full(conv_w.shape), _full(lnx_w.shape), _full(lnx_b.shape),
                  _full(e_bf16.shape), _full(wo_bf16.shape),
                  pl.BlockSpec((1, tm, d), row),
                  pl.BlockSpec((1, 1, d), vec)],
        out_specs=pl.BlockSpec((1, tm, d), row),
        out_shape=jax.ShapeDtypeStruct((nb, t, d), F32),
        compiler_params=_cparams("parallel", "parallel"),
        name="ev_out",
    )(pa, pa, pa, zb, yf, yb, bf, bb, conv_w, lnx_w, lnx_b, e_bf16, wo_bf16, x, gate)


def _head_rms(t, g):
    return t * lax.rsqrt(jnp.mean(t * t, axis=-1, keepdims=True) + NORM_EPS) * g


def _lane_rms_scale(t, ones_bf16, n):
    ss = jnp.dot((t * t).astype(BF16), ones_bf16, preferred_element_type=F32)
    return lax.rsqrt(ss * (1.0 / n) + NORM_EPS)


def _keys_values(p_kv, kvn, wkvb, gkn, gkr, gkr_sw, cos, sin, ones, k_ref, v_ref):
    kv_a = p_kv[:, :KV_LORA]
    kr = p_kv[:, KV_LORA:KV_LORA + QK_ROPE]
    kr_sw = p_kv[:, KV_LORA + QK_ROPE:KV_LORA + 2 * QK_ROPE]
    rs = lax.rsqrt(jnp.mean(kr * kr, axis=-1, keepdims=True) + NORM_EPS)
    if cos is None:
        k_rope = kr * rs * gkr
    else:
        k_rope = ((kr * gkr) * cos + (kr_sw * gkr_sw) * sin) * rs
    k_rope = k_rope.astype(BF16)
    kv = jnp.dot(_head_rms(kv_a, kvn).astype(BF16), wkvb, preferred_element_type=F32)
    for h in range(MLA_HEADS):
        kh = kv[:, h * QK_NOPE:(h + 1) * QK_NOPE]
        k_ref[0, h, :, 0:QK_NOPE] = (kh * _lane_rms_scale(kh, ones, QK_NOPE) * gkn).astype(BF16)
        k_ref[0, h, :, QK_NOPE:QK_DIM] = k_rope
        v_ref[0, h, :, 0:V_HEAD] = kv[:, MLA_WIDTH + h * V_HEAD:MLA_WIDTH + (h + 1) * V_HEAD].astype(BF16)
        v_ref[0, h, :, V_HEAD:V_COLS] = jnp.ones((kv.shape[0], V_HEAD), BF16)


def _od_lat_kernel(x_ref, sc_ref, sh_ref, nw_ref, win_ref, qan_ref, kvn_ref, wqb_ref, wkvb_ref,
                   gqn_ref, gqr_ref, gqrs_ref, gkn_ref, gkr_ref, gkrs_ref, ones_ref, e2_ref, cos_ref, sin_ref,
                   kc_ref, vc_ref, q_ref, k_ref, v_ref, z_ref, *, n_ctx_tiles):
    i = pl.program_id(1)

    @pl.when(i < n_ctx_tiles)
    def _():
        k_ref[...] = kc_ref[...]
        v_ref[...] = vc_ref[...]

    @pl.when(i >= n_ctx_tiles)
    def _():
        _od_lat_tile(x_ref, sc_ref, sh_ref, nw_ref, win_ref, qan_ref, kvn_ref, wqb_ref, wkvb_ref,
                     gqn_ref, gqr_ref, gqrs_ref, gkn_ref, gkr_ref, gkrs_ref, ones_ref, e2_ref, cos_ref, sin_ref,
                     q_ref, k_ref, v_ref, z_ref)


def _od_lat_tile(x_ref, sc_ref, sh_ref, nw_ref, win_ref, qan_ref, kvn_ref, wqb_ref, wkvb_ref,
                 gqn_ref, gqr_ref, gqrs_ref, gkn_ref, gkr_ref, gkrs_ref, ones_ref, e2_ref, cos_ref, sin_ref,
                 q_ref, k_ref, v_ref, z_ref):
    h = _modulated_norm(x_ref[0], nw_ref[...], sc_ref[0], sh_ref[0]).astype(BF16)
    p = jnp.dot(h, win_ref[...], preferred_element_type=F32)
    kv_off = Q_LORA
    z_off = Q_LORA + KV_LORA + 2 * QK_ROPE
    z_ref[0] = p[:, z_off:].astype(BF16)
    cos = cos_ref[...]
    sin = sin_ref[...]
    _keys_values(p[:, kv_off:z_off], kvn_ref[...], wkvb_ref[...], gkn_ref[...], gkr_ref[...], gkrs_ref[...],
                 cos[:, :QK_ROPE], sin[:, :QK_ROPE], ones_ref[...], k_ref, v_ref)

    q = jnp.dot(_head_rms(p[:, :Q_LORA], qan_ref[...]).astype(BF16), wqb_ref[...],
                preferred_element_type=F32)
    rw = MLA_HEADS * QK_ROPE
    qr = q[:, MLA_WIDTH:MLA_WIDTH + rw]
    qr_sw = q[:, MLA_WIDTH + rw:MLA_WIDTH + 2 * rw]
    rot = (qr * gqr_ref[...]) * cos + (qr_sw * gqrs_ref[...]) * sin
    ones = ones_ref[...]
    e2 = e2_ref[...]
    gqn = gqn_ref[...] * Q_SCALE
    rot = jnp.concatenate(
        [rot[:, j * LANES:(j + 1) * LANES] * (_lane_rms_scale(qr[:, j * LANES:(j + 1) * LANES], e2, QK_ROPE) * Q_SCALE)
         for j in range(rw // LANES)], axis=1).astype(BF16)
    for hd in range(MLA_HEADS):
        qh = q[:, hd * QK_NOPE:(hd + 1) * QK_NOPE]
        q_ref[0, hd, :, 0:QK_NOPE] = (qh * _lane_rms_scale(qh, ones, QK_NOPE) * gqn).astype(BF16)
        q_ref[0, hd, :, QK_NOPE:QK_DIM] = rot[:, hd * QK_ROPE:(hd + 1) * QK_ROPE]


def _od_ctx_kernel(x_ref, sc_ref, sh_ref, nw_ref, win_ref, kvn_ref, wkvb_ref,
                   gkn_ref, gkr_ref, gkrs_ref, ones_ref, k_ref, v_ref):
    h = _modulated_norm(x_ref[0], nw_ref[...], sc_ref[0], sh_ref[0]).astype(BF16)
    p = jnp.dot(h, win_ref[...], preferred_element_type=F32)
    _keys_values(p, kvn_ref[...], wkvb_ref[...], gkn_ref[...], gkr_ref[...], gkrs_ref[...],
                 None, None, ones_ref[...], k_ref, v_ref)


def _od_ctx_call(ctx, scale, shift, nw, win_kv, kvn, wkvb, gkn, gkr, gkrs, ones, tm):
    nb, tc, d = ctx.shape
    row = lambda b, i: (b, i, 0)
    vec = lambda b, i: (b, 0, 0)
    head = lambda b, i: (b, 0, i, 0)
    params = [nw, win_kv, kvn, wkvb, gkn, gkr, gkrs, ones]
    return pl.pallas_call(
        _od_ctx_kernel,
        grid=(nb, tc // tm),
        in_specs=[pl.BlockSpec((1, tm, d), row), pl.BlockSpec((1, 1, d), vec), pl.BlockSpec((1, 1, d), vec)]
                 + [_full(p.shape) for p in params],
        out_specs=[pl.BlockSpec((1, MLA_HEADS, tm, QK_DIM), head),
                   pl.BlockSpec((1, MLA_HEADS, tm, V_COLS), head)],
        out_shape=[jax.ShapeDtypeStruct((nb, MLA_HEADS, tc, QK_DIM), BF16),
                   jax.ShapeDtypeStruct((nb, MLA_HEADS, tc, V_COLS), BF16)],
        compiler_params=_cparams("parallel", "parallel"),
        name="od_in_ctx",
    )(ctx, scale, shift, *params)


def _od_lat_call(x, scale, shift, params, cos, sin, k_ctx, v_ctx, tm):
    nb, t, d = x.shape
    tc = k_ctx.shape[2]
    assert tc % tm == 0
    n_off = tc // tm
    lat = lambda i: jnp.maximum(i - n_off, 0)
    cxt = lambda i: jnp.minimum(i, n_off - 1)
    row = lambda b, i: (b, lat(i), 0)
    vec = lambda b, i: (b, 0, 0)
    head_q = lambda b, i: (b, 0, lat(i), 0)
    head_c = lambda b, i: (b, 0, cxt(i), 0)
    head_k = lambda b, i: (b, 0, i, 0)
    tab = lambda b, i: (lat(i), 0)
    rw = MLA_HEADS * QK_ROPE
    return pl.pallas_call(
        functools.partial(_od_lat_kernel, n_ctx_tiles=n_off),
        grid=(nb, n_off + t // tm),
        in_specs=[pl.BlockSpec((1, tm, d), row), pl.BlockSpec((1, 1, d), vec), pl.BlockSpec((1, 1, d), vec)]
                 + [_full(p.shape) for p in params]
                 + [pl.BlockSpec((tm, rw), tab), pl.BlockSpec((tm, rw), tab),
                    pl.BlockSpec((1, MLA_HEADS, tm, QK_DIM), head_c),
                    pl.BlockSpec((1, MLA_HEADS, tm, V_COLS), head_c)],
        out_specs=[pl.BlockSpec((1, MLA_HEADS, tm, QK_DIM), head_q),
                   pl.BlockSpec((1, MLA_HEADS, tm, QK_DIM), head_k),
                   pl.BlockSpec((1, MLA_HEADS, tm, V_COLS), head_k),
                   pl.BlockSpec((1, tm, MLA_WIDTH), row)],
        out_shape=[jax.ShapeDtypeStruct((nb, MLA_HEADS, t, QK_DIM), BF16),
                   jax.ShapeDtypeStruct((nb, MLA_HEADS, tc + t, QK_DIM), BF16),
                   jax.ShapeDtypeStruct((nb, MLA_HEADS, tc + t, V_COLS), BF16),
                   jax.ShapeDtypeStruct((nb, t, MLA_WIDTH), BF16)],
        compiler_params=_cparams("parallel", "arbitrary"),
        name="od_in_lat",
    )(x, scale, shift, *params, cos, sin, k_ctx, v_ctx)


ATTN_SUB = 512


def _attn_kernel(q_ref, k_ref, v_ref, o_ref):
    k = k_ref[0, 0]
    v = v_ref[0, 0]
    sub = min(ATTN_SUB, q_ref.shape[2])
    n_sub = q_ref.shape[2] // sub

    def scores(i):
        return lax.dot_general(q_ref[0, 0, i * sub:(i + 1) * sub, :], k, (((1,), (1,)), ((), ())),
                               preferred_element_type=F32)

    s = scores(0)
    for i in range(n_sub):
        s_next = scores(i + 1) if i + 1 < n_sub else None
        p = jnp.exp2((s - jnp.max(s, axis=-1, keepdims=True)).astype(BF16))
        ov = jnp.dot(p, v, preferred_element_type=F32)
        o_ref[0, i * sub:(i + 1) * sub, :] = (ov[:, :V_HEAD] / ov[:, V_HEAD:]).astype(BF16)
        s = s_next


def _attn_call(q, k, v, tq):
    nb, nh, t, _ = q.shape
    tk = k.shape[2]
    whole = lambda b, h, i: (b, h, 0, 0)
    return pl.pallas_call(
        _attn_kernel,
        grid=(nb, nh, t // tq),
        in_specs=[pl.BlockSpec((1, 1, tq, QK_DIM), lambda b, h, i: (b, h, i, 0)),
                  pl.BlockSpec((1, 1, tk, QK_DIM), whole),
                  pl.BlockSpec((1, 1, tk, V_COLS), whole)],
        out_specs=pl.BlockSpec((1, tq, V_HEAD), lambda b, h, i: (b, i, h)),
        out_shape=jax.ShapeDtypeStruct((nb, t, nh * V_HEAD), BF16),
        compiler_params=_cparams("parallel", "parallel", "parallel"),
        name="mla_attn",
    )(q, k, v)


def _od_out_kernel(o_ref, z_ref, wo_ref, x_ref, g_ref, out_ref):
    a = (o_ref[0].astype(F32) * _silu(z_ref[0].astype(F32))).astype(BF16)
    out_ref[0] = x_ref[0] + g_ref[0] * jnp.dot(a, wo_ref[...], preferred_element_type=F32)


def _od_out_call(o, z, wo_bf16, x, gate, tm):
    nb, t, d = x.shape
    row = lambda b, i: (b, i, 0)
    return pl.pallas_call(
        _od_out_kernel,
        grid=(nb, t // tm),
        in_specs=[pl.BlockSpec((1, tm, MLA_WIDTH), row), pl.BlockSpec((1, tm, MLA_WIDTH), row),
                  _full(wo_bf16.shape), pl.BlockSpec((1, tm, d), row),
                  pl.BlockSpec((1, 1, d), lambda b, i: (b, 0, 0))],
        out_specs=pl.BlockSpec((1, tm, d), row),
        out_shape=jax.ShapeDtypeStruct((nb, t, d), F32),
        compiler_params=_cparams("parallel", "parallel"),
        name="od_out",
    )(o, z, wo_bf16, x, gate)


def _rope_tables(n):
    rows = n // GRID_W
    row = jnp.repeat(jnp.arange(rows, dtype=F32), GRID_W)
    col = jnp.tile(jnp.arange(GRID_W, dtype=F32), rows)
    inv_freq = ROPE_THETA ** (-jnp.arange(ROPE_PAIRS, dtype=F32) / ROPE_PAIRS)
    ar = row[:, None] * inv_freq
    ac = col[:, None] * inv_freq
    cos = jnp.concatenate([jnp.cos(ar), jnp.cos(ar), jnp.cos(ac), jnp.cos(ac)], axis=1)
    sin = jnp.concatenate([-jnp.sin(ar), jnp.sin(ar), -jnp.sin(ac), jnp.sin(ac)], axis=1)
    return cos, sin


def _row_tile(t, want):
    tm = min(t, want)
    assert t % tm == 0 and tm % HALO == 0
    return tm


def kernel(x, c, ctx, c_ctx, ada_w, ada_b, norm_w, ev_w_in, ev_conv_w, ev_mu, ev_k_k, ev_k_a, ev_w0, ev_w2, ev_a0, ev_a2, ev_r_k, ev_lnx_w, ev_lnx_b, ev_w_out, od_w_in, od_q_a_norm, od_kv_a_norm, od_w_qb, od_w_kvb, od_gq_nope, od_gq_rope, od_gk_nope, od_gk_rope, od_w_o):
    nb, t, d = x.shape
    tc = ctx.shape[1]
    assert t % CHUNK == 0 and tc % CHUNK == 0 and t % GRID_W == 0

    rows = -(-(nb + 1) // HALO) * HALO
    cc = jnp.zeros((rows, d), F32).at[:nb].set(c).at[nb].set(c_ctx)
    mod = _ada_call(cc, ada_w, ada_b)

    def mods(layer):
        m = mod[layer]
        lat = [m[:nb, i * d:(i + 1) * d].reshape(nb, 1, d) for i in range(3)]
        cx = [jnp.broadcast_to(m[nb, i * d:(i + 1) * d].reshape(1, 1, d), (nb, 1, d)) for i in range(3)]
        return lat, cx

    e_np = np.arange(PAIR)[:, None] // RWKV_HEAD == np.arange(PAIR)[None, :] // RWKV_HEAD
    e_bf16 = jnp.asarray(e_np, BF16)
    ones_bf16 = jnp.ones((LANES, LANES), BF16)

    (sh_l, sc_l, g_l), (sh_c, sc_c, g_c) = mods(0)
    nw0 = norm_w[0].reshape(1, d)
    w_in0 = ev_w_in[0].astype(BF16)
    tm_l = _row_tile(t, 256)
    tm_c = _row_tile(tc, 256)
    pa_c, pr_c, zb_c = _ev_in_call(ctx, sc_c, sh_c, nw0, w_in0, tm_c)
    pa_l, pr_l, zb_l = _ev_in_call(x, sc_l, sh_l, nw0, w_in0, tm_l)

    W = RWKV_WIDTH
    vec = lambda a: a.reshape(1, -1)
    lora = jnp.zeros((2, 2 * LORA, 2 * W), F32)
    lora = lora.at[:, :LORA, :W].set(ev_w2[0]).at[:, LORA:, W:].set(ev_a2[0])
    lora_hi = lora.astype(BF16)
    lora_lo = (lora - lora_hi.astype(F32)).astype(BF16)
    wa0 = jnp.concatenate([ev_w0[0], ev_a0[0]], axis=-1).reshape(2, 1, 2 * W)
    params = [vec(1.0 - ev_mu[0]), vec(0.5 * ev_mu[0]), vec(ev_k_k[0]), vec(ev_k_a[0]), vec(1.0 - ev_k_a[0]),
              vec(ev_r_k[0]), wa0, lora_hi, lora_lo, e_bf16]
    yf_c, yb_c, bf_c, bb_c, s_c = _scan_call(pr_c, params, None)
    yf_l, yb_l, bf_l, bb_l, _ = _scan_call(pr_l, params, s_c)

    wo0 = ev_w_out[0].astype(BF16)
    lw, lb = vec(ev_lnx_w[0]), vec(ev_lnx_b[0])
    x1 = _ev_out_call(pa_l, zb_l, yf_l, yb_l, bf_l, bb_l, ev_conv_w[0], lw, lb, e_bf16, wo0, x, g_l,
                      _row_tile(t, 512))
    ctx1 = _ev_out_call(pa_c, zb_c, yf_c, yb_c, bf_c, bb_c, ev_conv_w[0], lw, lb, e_bf16, wo0, ctx, g_c, tm_c)

    (sh_l, sc_l, g_l), (sh_c, sc_c, _) = mods(1)
    nw1 = norm_w[1].reshape(1, d)
    swap = np.arange(QK_ROPE) ^ ROPE_PAIRS
    hq = np.arange(MLA_HEADS)[:, None] * QK_DIM
    idx_q = np.concatenate([(hq + np.arange(QK_NOPE)).ravel(),
                            (hq + QK_NOPE + np.arange(QK_ROPE)).ravel(),
                            (hq + QK_NOPE + swap).ravel()])
    hk = np.arange(MLA_HEADS)[:, None] * (QK_NOPE + V_HEAD)
    idx_kv = np.concatenate([(hk + np.arange(QK_NOPE)).ravel(), (hk + QK_NOPE + np.arange(V_HEAD)).ravel()])
    r_off = Q_LORA + KV_LORA
    idx_in = np.concatenate([np.arange(r_off + QK_ROPE), r_off + swap,
                             np.arange(r_off + QK_ROPE, od_w_in.shape[2])])
    w_in1 = od_w_in[0][:, idx_in].astype(BF16)
    wqb = od_w_qb[0][:, idx_q].astype(BF16)
    wkvb = od_w_kvb[0][:, idx_kv].astype(BF16)
    gqr = jnp.tile(od_gq_rope[0], MLA_HEADS).reshape(1, -1)
    gqrs = jnp.tile(od_gq_rope[0][swap], MLA_HEADS).reshape(1, -1)
    gkr, gkrs = vec(od_gk_rope[0]), vec(od_gk_rope[0][swap])
    gqn, gkn = vec(od_gq_nope[0]), vec(od_gk_nope[0])
    qan, kvn = vec(od_q_a_norm[0]), vec(od_kv_a_norm[0])
    cos, sin = _rope_tables(t)
    cos = jnp.tile(cos, (1, MLA_HEADS))
    sin = jnp.tile(sin, (1, MLA_HEADS))

    k_c, v_c = _od_ctx_call(ctx1, sc_c, sh_c, nw1, w_in1[:, Q_LORA:Q_LORA + KV_LORA + 2 * QK_ROPE],
                            kvn, wkvb, gkn, gkr, gkrs, ones_bf16, _row_tile(tc, 256))
    params = [nw1, w_in1, qan, kvn, wqb, wkvb, gqn, gqr, gqrs, gkn, gkr, gkrs, ones_bf16, e_bf16]
    q, k_all, v_all, z = _od_lat_call(x1, sc_l, sh_l, params, cos, sin, k_c, v_c, _row_tile(tc, 256))
    o = _attn_call(q, k_all, v_all, _row_tile(t, 2 * ATTN_SUB))
    return _od_out_call(o, z, od_w_o[0].astype(BF16), x1, g_l, _row_tile(t, 512))
```

```python
import functools

import numpy as np
import jax
import jax.numpy as jnp
from jax import lax
from jax.experimental import pallas as pl
from jax.experimental.pallas import tpu as pltpu

F32 = jnp.float32
BF16 = jnp.bfloat16

NORM_EPS = 1e-6
GRID_W = 64

CONV_WIDTH = 512
RWKV_HEAD = 64
RWKV_WIDTH = 512
LORA = 64
RWKV_GN_EPS = 64e-5
DECAY_SCALE = -float(np.exp(-0.5))
A_COLS = 4 * CONV_WIDTH
SHIFT_COLS = 3 * RWKV_WIDTH + 2 * LORA
CHUNK = 64
PAIR = 2 * RWKV_HEAD
N_PAIRS = RWKV_WIDTH // PAIR

MLA_HEADS = 8
QK_NOPE = 128
QK_ROPE = 64
QK_DIM = QK_NOPE + QK_ROPE
V_HEAD = 128
Q_LORA = 384
KV_LORA = 256
MLA_WIDTH = MLA_HEADS * V_HEAD
SM_SCALE = QK_DIM ** -0.5
LOG2E = 1.4426950408889634
Q_SCALE = SM_SCALE * LOG2E
V_COLS = 2 * V_HEAD
ROPE_THETA = 10000.0
ROPE_PAIRS = QK_ROPE // 4
LANES = 128

VMEM_LIMIT_BYTES = 56 * 1024 * 1024
HALO = 8
HALO_BF16 = 16


def _cparams(*sem):
    return pltpu.CompilerParams(dimension_semantics=sem, vmem_limit_bytes=VMEM_LIMIT_BYTES)


def _silu(x):
    return x / (1.0 + jnp.exp(-x))


def _dot(a, b):
    return jnp.dot(a.astype(BF16), b.astype(BF16), preferred_element_type=F32)


def _dot_nt(a, b):
    return lax.dot_general(a.astype(BF16), b.astype(BF16), (((1,), (1,)), ((), ())),
                           preferred_element_type=F32)


def _bf16_split(x):
    hi = x.astype(BF16)
    return hi, (x - hi.astype(F32)).astype(BF16)


def _mm(a, b):
    return jnp.dot(a, b, preferred_element_type=F32)


def _dot3(a, b_hi, b_lo):
    a_hi, a_lo = _bf16_split(a)
    return _mm(a_hi, b_hi) + (_mm(a_lo, b_hi) + _mm(a_hi, b_lo))


def _exact_lhs_dot(m_bf16, x):
    hi, lo = _bf16_split(x)
    return _mm(m_bf16, hi) + _mm(m_bf16, lo)


def _segsum(x, e2, passes=2):
    out = []
    for i in range(x.shape[1] // PAIR):
        hi, lo = _bf16_split(x[:, i * PAIR:(i + 1) * PAIR])
        out.append(_mm(hi, e2) + _mm(lo, e2) if passes == 2 else _mm(hi, e2))
    return jnp.concatenate(out, axis=1)


def _full(shape):
    return pl.BlockSpec(shape, lambda *_: (0,) * len(shape))


def _ada_kernel(c_ref, w_ref, b_ref, o_ref):
    o_ref[0] = _dot3(_silu(c_ref[...]), *_bf16_split(w_ref[0])) + b_ref[0]


def _ada_call(cc, ada_w, ada_b):
    depth, d, d3 = ada_w.shape
    rows = cc.shape[0]
    tn = 1024
    return pl.pallas_call(
        _ada_kernel,
        grid=(depth, d3 // tn),
        in_specs=[pl.BlockSpec((rows, d), lambda l, j: (0, 0)),
                  pl.BlockSpec((1, d, tn), lambda l, j: (l, 0, j)),
                  pl.BlockSpec((1, 1, tn), lambda l, j: (l, 0, j))],
        out_specs=pl.BlockSpec((1, rows, tn), lambda l, j: (l, 0, j)),
        out_shape=jax.ShapeDtypeStruct((depth, rows, d3), F32),
        compiler_params=_cparams("arbitrary", "arbitrary"),
        name="ada_mod",
    )(cc, ada_w, ada_b.reshape(depth, 1, d3))


def _modulated_norm(x, nw, scale, shift):
    rs = lax.rsqrt(jnp.mean(x * x, axis=-1, keepdims=True) + NORM_EPS)
    return (x * rs * nw) * (1.0 + scale) + shift


def _ev_in_kernel(x_ref, sc_ref, sh_ref, nw_ref, w_ref, pa_ref, pr_ref, zb_ref):
    h = _modulated_norm(x_ref[0], nw_ref[...], sc_ref[0], sh_ref[0]).astype(BF16)
    pa_ref[0] = jnp.dot(h, w_ref[:, :A_COLS], preferred_element_type=F32).astype(BF16)
    pr_ref[0] = jnp.dot(h, w_ref[:, A_COLS:A_COLS + SHIFT_COLS], preferred_element_type=F32)
    zb_ref[0] = jnp.dot(h, w_ref[:, A_COLS + SHIFT_COLS:], preferred_element_type=F32).astype(BF16)


def _ev_in_call(x, scale, shift, nw, w_bf16, tm):
    nb, t, d = x.shape
    ncol = w_bf16.shape[1]
    row = lambda b, i: (b, i, 0)
    vec = lambda b, i: (b, 0, 0)
    return pl.pallas_call(
        _ev_in_kernel,
        grid=(nb, t // tm),
        in_specs=[pl.BlockSpec((1, tm, d), row),
                  pl.BlockSpec((1, 1, d), vec),
                  pl.BlockSpec((1, 1, d), vec),
                  _full((1, d)),
                  _full((d, ncol))],
        out_specs=[pl.BlockSpec((1, tm, A_COLS), row),
                   pl.BlockSpec((1, tm, SHIFT_COLS), row),
                   pl.BlockSpec((1, tm, RWKV_WIDTH), row)],
        out_shape=[jax.ShapeDtypeStruct((nb, t, A_COLS), BF16),
                   jax.ShapeDtypeStruct((nb, t, SHIFT_COLS), F32),
                   jax.ShapeDtypeStruct((nb, t, RWKV_WIDTH), BF16)],
        compiler_params=_cparams("parallel", "parallel"),
        name="ev_in",
    )(x, scale, shift, nw, w_bf16)


def _neighbours(cur, prev_halo, next_halo, is_first, is_last):
    n = cur.shape[0]
    rowi = lax.broadcasted_iota(jnp.int32, cur.shape, 0)
    prev_row = jnp.where(is_first, 0.0, prev_halo[prev_halo.shape[0] - 1:, :])
    next_row = jnp.where(is_last, 0.0, next_halo[0:1, :])
    prev = jnp.where(rowi == 0, prev_row, pltpu.roll(cur, 1, 0))
    nxt = jnp.where(rowi == n - 1, next_row, pltpu.roll(cur, n - 1, 0))
    return prev, nxt


OPERANDS = ("rh", "ah", "bc", "kc", "bt", "kt", "v", "bon")


def _scan_prep_pieces(reverse, cur_ref, prv_ref, nxt_ref, is_first, is_last, prm, w_dir, ops_ref, gc_ref, d):
    C = CHUNK
    W = RWKV_WIDTH
    mu_keep, mu_half, k_k, k_a, ka_keep, r_k, e2 = prm
    wa0, lora_hi, lora_lo = w_dir
    t = {}

    def shift():
        pr = cur_ref[0]
        prev, nxt = _neighbours(pr, prv_ref[0], nxt_ref[0], is_first, is_last)
        x = mu_keep * pr + mu_half * (prev + nxt)
        t.update(r=x[:, 0:W], k=x[:, W:2 * W], v=x[:, 2 * W:3 * W], wa=x[:, 3 * W:3 * W + 2 * LORA])

    def rates():
        lane = lax.broadcasted_iota(jnp.int32, (C, PAIR), 1)
        tw = jnp.where(lane < LORA, jnp.tanh(t["wa"]), t["wa"])
        lo = _dot3(tw, lora_hi, lora_lo)
        sig = 1.0 / (1.0 + jnp.exp(-(wa0 + lo)))
        t.update(lw=DECAY_SCALE * sig[:, :W], iclr=sig[:, W:])

    def keys():
        k, iclr = t["k"], t["iclr"]
        kkv = k * k_k
        kkn = kkv * lax.rsqrt(jnp.maximum(_segsum(kkv * kkv, e2, passes=1), 1e-24))
        k_dir = k * (ka_keep + iclr * k_a)
        t.update(kkn=kkn, k_dir=k_dir, bvec=kkn * iclr)
        ops_ref[d, OPERANDS.index("bon")] = _segsum(t["r"] * k_dir * r_k, e2) * t["v"]

    def decays():
        lw = t["lw"]
        ti = lax.broadcasted_iota(jnp.int32, (C, C), 0)
        si = lax.broadcasted_iota(jnp.int32, (C, C), 1)
        tri = ((si >= ti) if reverse else (si <= ti)).astype(BF16)
        g_in = _exact_lhs_dot(tri, lw)
        g_tot = jnp.sum(lw, axis=0, keepdims=True)
        t.update(gam=jnp.exp(g_in), gam_ex=jnp.exp(g_in - lw), gam_inv=jnp.exp(-g_in), gam_c=jnp.exp(g_tot))

    def store():
        gam_c = t["gam_c"]
        bc = t["bvec"] * t["gam_inv"]
        kc = t["k_dir"] * t["gam_inv"]
        out = dict(rh=t["r"] * t["gam"], ah=-t["kkn"] * t["gam_ex"], bc=bc, kc=kc,
                   bt=bc * gam_c, kt=kc * gam_c, v=t["v"])
        for name, val in out.items():
            ops_ref[d, OPERANDS.index(name)] = val
        gc_ref[d] = gam_c

    return [shift, rates, keys, decays, store]


def _scan_kernel(n_chunks, has_s0, *refs):
    refs = list(refs)
    first_refs = (refs[0:3], refs[3:6])
    next_refs = (refs[6:9], refs[9:12])
    (muk_ref, muh_ref, kk_ref, ka_ref, kak_ref, rk_ref, wa0_ref, lh_ref, ll_ref, e_ref) = refs[12:22]
    pos = 22
    s0_ref = None
    if has_s0:
        s0_ref = refs[pos]
        pos += 1
    y_refs = refs[pos:pos + 2]
    bon_refs = refs[pos + 2:pos + 4]
    sfin_ref, st_ref, ops_a, gc_a, ops_b, gc_b = refs[pos + 4:pos + 10]

    C = CHUNK
    c = pl.program_id(1)
    prm = (muk_ref[...], muh_ref[...], kk_ref[...], ka_ref[...], kak_ref[...], rk_ref[...], e_ref[...])

    def pieces(d, src, is_first, is_last, ops_dst, gc_dst):
        cur, prv, nxt = src
        return _scan_prep_pieces(d == 1, cur, prv, nxt, is_first, is_last, prm,
                                 (wa0_ref[d], lh_ref[d], ll_ref[d]), ops_dst, gc_dst, d)

    @pl.when(c == 0)
    def _():
        if has_s0:
            st_ref[...] = s0_ref[0]
        else:
            st_ref[...] = jnp.zeros_like(st_ref)
        for piece in pieces(0, first_refs[0], True, n_chunks == 1, ops_a, gc_a) + \
                pieces(1, first_refs[1], n_chunks == 1, True, ops_a, gc_a):
            piece()

    nxt_f = jnp.minimum(c + 1, n_chunks - 1)
    nxt_b = jnp.maximum(n_chunks - 2 - c, 0)

    def chain(ops_ref, gc_ref, overlap):
        n2 = 2 * C
        ri = lax.broadcasted_iota(jnp.int32, (n2, n2), 0)
        ci = lax.broadcasted_iota(jnp.int32, (n2, n2), 1)
        same_head = (ri >= C) == (ci >= C)
        rt = ri & (C - 1)
        ct = ci & (C - 1)
        strict = (same_head & (ct < rt), same_head & (ct > rt))
        incl = (same_head & (ct <= rt), same_head & (ct >= rt))
        diag = ri == ci
        eye = diag.astype(F32)
        blocks = [same_head & ((rt >> s) == (ct >> s)) for s in range(1, C.bit_length())]
        first_head = lax.broadcasted_iota(jnp.int32, (C, PAIR), 1) < RWKV_HEAD

        def stk(z):
            return jnp.concatenate([jnp.where(first_head, z, 0.0), jnp.where(first_head, 0.0, z)], axis=0)

        def dup(z):
            return jnp.concatenate([z, z], axis=0)

        chains = [(d, slice(p * PAIR, (p + 1) * PAIR), p) for d in range(2) for p in range(N_PAIRS)]
        n_ch = len(chains)

        def operand(name, d, sl):
            return ops_ref[d, OPERANDS.index(name), :, sl]

        ah_s = [stk(operand("ah", d, sl)) for d, sl, _ in chains]
        rh_s = [stk(operand("rh", d, sl)) for d, sl, _ in chains]
        v_s = [stk(operand("v", d, sl)) for d, sl, _ in chains]
        pm = [_dot_nt(jnp.concatenate([ah_s[i], rh_s[i]], axis=0),
                      jnp.concatenate([dup(operand("bc", d, sl)), dup(operand("kc", d, sl))], axis=0))
              for i, (d, sl, _) in enumerate(chains)]
        overlap()
        a_ab = [jnp.where(strict[d], pm[i][:n2, :n2], 0.0) for i, (d, _, _) in enumerate(chains)]
        a_ak = [jnp.where(strict[d], pm[i][:n2, n2:], 0.0) for i, (d, _, _) in enumerate(chains)]
        a_r = [jnp.concatenate([jnp.where(incl[d], pm[i][n2:, :n2], 0.0),
                                jnp.where(incl[d], pm[i][n2:, n2:], 0.0)], axis=1)
               for i, (d, _, _) in enumerate(chains)]
        akv = [_dot(a_ak[i], v_s[i]) for i in range(n_ch)]
        tinv = [jnp.where(blocks[0], a, 0.0) + eye for a in a_ab]
        overlap()
        for lvl in range(1, len(blocks)):
            off = blocks[lvl] & ~blocks[lvl - 1]
            s = 1 << lvl
            if s < HALO:
                t1 = [_dot(jnp.where(off, a_ab[i], 0.0), tinv[i]) for i in range(n_ch)]
                overlap()
                t2 = [_dot(tinv[i], t1[i]) for i in range(n_ch)]
                tinv = [tinv[i] + t2[i] for i in range(n_ch)]
            else:
                def band(x, d):
                    return jnp.concatenate([x[j * s:(j + 1) * s] for j in range(1 - d, n2 // s, 2)], axis=0)

                def unband(x, d, rest):
                    return jnp.concatenate([x[(j // 2) * s:(j // 2 + 1) * s] if j % 2 == 1 - d
                                            else rest[j * s:(j + 1) * s] for j in range(n2 // s)], axis=0)

                t1 = [_dot(band(jnp.where(off, a_ab[i], 0.0), d), tinv[i]) for i, (d, _, _) in enumerate(chains)]
                overlap()
                zero = jnp.zeros((n2, n2), F32)
                t2 = [_dot(band(tinv[i], d), unband(t1[i], d, zero)) for i, (d, _, _) in enumerate(chains)]
                tinv = [unband(band(tinv[i], d) + t2[i], d, tinv[i]) for i, (d, _, _) in enumerate(chains)]
            overlap()
        xx = [_dot(tinv[i], jnp.concatenate([ah_s[i], akv[i]], axis=1)) for i in range(n_ch)]
        zeros = jnp.zeros((n2, PAIR), F32)
        rr = []
        for i, (d, sl, _) in enumerate(chains):
            rhs = jnp.concatenate([xx[i], jnp.concatenate([zeros, v_s[i]], axis=1)], axis=0)
            lhs = jnp.concatenate(
                [a_r[i], jnp.concatenate([stk(operand("bt", d, sl)).T, stk(operand("kt", d, sl)).T], axis=1)], axis=0)
            rr.append(_dot(lhs, rhs))
        for i, (d, sl, p) in enumerate(chains):
            q_t = rh_s[i] + rr[i][:n2, :PAIR]
            st = st_ref[d, p]
            st_hi, st_lo = _bf16_split(st)
            m_off = rr[i][n2:, :PAIR].astype(BF16)
            qm = _mm(jnp.concatenate([q_t.astype(BF16), m_off], axis=0), st_hi)
            ys = qm[:n2] + rr[i][:n2, PAIR:]
            g_col = jnp.sum(jnp.where(diag, gc_ref[d][:, sl], 0.0), axis=1, keepdims=True)
            st_ref[d, p] = (g_col * st + (qm[n2:] + _mm(m_off, st_lo))) + rr[i][n2:, PAIR:]
            y_refs[d][0, :, sl] = ys[:C] + ys[C:]
        for d in range(2):
            bon_refs[d][0] = ops_ref[d, OPERANDS.index("bon")]

    def step(ops_ref, gc_ref, ops_nxt, gc_nxt):
        prep_f = pieces(0, next_refs[0], nxt_f == 0, nxt_f == n_chunks - 1, ops_nxt, gc_nxt)
        prep_b = pieces(1, next_refs[1], nxt_b == 0, nxt_b == n_chunks - 1, ops_nxt, gc_nxt)
        background = [fn for pair in zip(prep_f, prep_b) for fn in pair]

        def overlap():
            if background:
                background.pop(0)()

        chain(ops_ref, gc_ref, overlap)
        while background:
            overlap()

    @pl.when(c % 2 == 0)
    def _():
        step(ops_a, gc_a, ops_b, gc_b)

    @pl.when(c % 2 == 1)
    def _():
        step(ops_b, gc_b, ops_a, gc_a)

    @pl.when(c == n_chunks - 1)
    def _():
        sfin_ref[0] = st_ref[...]


def _scan_call(pr, params, s0):
    nb, t, _ = pr.shape
    C = CHUNK
    n_chunks = t // C
    hb = C // HALO
    n_hblk = t // HALO

    def specs(cs_of):
        return [pl.BlockSpec((1, C, SHIFT_COLS), lambda b, c: (b, cs_of(c), 0)),
                pl.BlockSpec((1, HALO, SHIFT_COLS), lambda b, c: (b, jnp.maximum(cs_of(c) * hb - 1, 0), 0)),
                pl.BlockSpec((1, HALO, SHIFT_COLS), lambda b, c: (b, jnp.minimum((cs_of(c) + 1) * hb, n_hblk - 1), 0))]

    first_f = lambda c: 0
    first_b = lambda c: n_chunks - 1
    next_f = lambda c: jnp.minimum(c + 1, n_chunks - 1)
    next_b = lambda c: jnp.maximum(n_chunks - 2 - c, 0)
    st_spec = pl.BlockSpec((1, 2, N_PAIRS, PAIR, PAIR), lambda b, c: (b, 0, 0, 0, 0))
    in_specs = specs(first_f) + specs(first_b) + specs(next_f) + specs(next_b) + [_full(p.shape) for p in params]
    args = [pr] * 12 + list(params)
    if s0 is not None:
        in_specs.append(st_spec)
        args.append(s0)
    out_f = pl.BlockSpec((1, C, RWKV_WIDTH), lambda b, c: (b, c, 0))
    out_b = pl.BlockSpec((1, C, RWKV_WIDTH), lambda b, c: (b, n_chunks - 1 - c, 0))
    tok = jax.ShapeDtypeStruct((nb, t, RWKV_WIDTH), F32)
    return pl.pallas_call(
        functools.partial(_scan_kernel, n_chunks, s0 is not None),
        grid=(nb, n_chunks),
        in_specs=in_specs,
        out_specs=[out_f, out_b, out_f, out_b, st_spec],
        out_shape=[tok, tok, tok, tok, jax.ShapeDtypeStruct((nb, 2, N_PAIRS, PAIR, PAIR), F32)],
        scratch_shapes=[pltpu.VMEM((2, N_PAIRS, PAIR, PAIR), F32),
                        pltpu.VMEM((2, len(OPERANDS), C, RWKV_WIDTH), F32), pltpu.VMEM((2, 1, RWKV_WIDTH), F32),
                        pltpu.VMEM((2, len(OPERANDS), C, RWKV_WIDTH), F32), pltpu.VMEM((2, 1, RWKV_WIDTH), F32)],
        compiler_params=_cparams("parallel", "arbitrary"),
        name="rwkv_scan",
    )(*args)


def _ev_out_kernel(n_tiles, pa_ref, pap_ref, pan_ref, zb_ref, yf_ref, yb_ref, bf_ref, bb_ref,
                   cw_ref, lw_ref, lb_ref, e_ref, wo_ref, x_ref, g_ref, o_ref):
    i = pl.program_id(1)
    W = CONV_WIDTH
    pa = pa_ref[0].astype(F32)
    u, gate_b, gate_c, z = pa[:, 0:W], pa[:, W:2 * W], pa[:, 2 * W:3 * W], pa[:, 3 * W:4 * W]
    cu = gate_c * u
    pa_p = pap_ref[0].astype(F32)
    pa_n = pan_ref[0].astype(F32)
    cu_p = pa_p[:, 2 * W:3 * W] * pa_p[:, 0:W]
    cu_n = pa_n[:, 2 * W:3 * W] * pa_n[:, 0:W]
    prev, nxt = _neighbours(cu, cu_p, cu_n, i == 0, i == n_tiles - 1)
    cw = cw_ref[...]
    conv = cw[0:1] * prev + cw[1:2] * cu + cw[2:3] * nxt
    a_out = gate_b * conv * _silu(z)

    e = e_ref[...]
    ysum = yf_ref[0] + yb_ref[0]
    inv_n = 1.0 / RWKV_HEAD
    mean = _segsum(ysum, e) * inv_n
    dlt = ysum - mean
    var = _segsum(dlt * dlt, e) * inv_n
    rw = dlt * lax.rsqrt(var + RWKV_GN_EPS) * lw_ref[...] + lb_ref[...] + (bf_ref[0] + bb_ref[0])
    b_out = rw * _silu(zb_ref[0].astype(F32))

    y = (jnp.dot(a_out.astype(BF16), wo_ref[:W, :], preferred_element_type=F32)
         + jnp.dot(b_out.astype(BF16), wo_ref[W:, :], preferred_element_type=F32))
    o_ref[0] = x_ref[0] + g_ref[0] * y


def _ev_out_call(pa, zb, yf, yb, bf, bb, conv_w, lnx_w, lnx_b, e_bf16, wo_bf16, x, gate, tm):
    nb, t, d = x.shape
    n_tiles = t // tm
    hb = tm // HALO_BF16
    n_hblk = t // HALO_BF16
    row = lambda b, i: (b, i, 0)
    prv = lambda b, i: (b, jnp.maximum(i * hb - 1, 0), 0)
    nxt = lambda b, i: (b, jnp.minimum((i + 1) * hb, n_hblk - 1), 0)
    vec = lambda b, i: (b, 0, 0)
    wspec = pl.BlockSpec((1, tm, RWKV_WIDTH), row)
    return pl.pallas_call(
        functools.partial(_ev_out_kernel, n_tiles),
        grid=(nb, n_tiles),
        in_specs=[pl.BlockSpec((1, tm, A_COLS), row),
                  pl.BlockSpec((1, HALO_BF16, A_COLS), prv),
                  pl.BlockSpec((1, HALO_BF16, A_COLS), nxt),
                  wspec, wspec, wspec, wspec, wspec,
                  _full(conv_w.shape), _full(lnx_w.shape), _full(lnx_b.shape),
                  _full(e_bf16.shape), _full(wo_bf16.shape),
                  pl.BlockSpec((1, tm, d), row),
                  pl.BlockSpec((1, 1, d), vec)],
        out_specs=pl.BlockSpec((1, tm, d), row),
        out_shape=jax.ShapeDtypeStruct((nb, t, d), F32),
        compiler_params=_cparams("parallel", "parallel"),
        name="ev_out",
    )(pa, pa, pa, zb, yf, yb, bf, bb, conv_w, lnx_w, lnx_b, e_bf16, wo_bf16, x, gate)


def _head_rms(t, g):
    return t * lax.rsqrt(jnp.mean(t * t, axis=-1, keepdims=True) + NORM_EPS) * g


def _lane_rms_scale(t, ones_bf16, n):
    ss = jnp.dot((t * t).astype(BF16), ones_bf16, preferred_element_type=F32)
    return lax.rsqrt(ss * (1.0 / n) + NORM_EPS)


def _keys_values(p_kv, kvn, wkvb, gkn, gkr, gkr_sw, cos, sin, ones, k_ref, v_ref):
    kv_a = p_kv[:, :KV_LORA]
    kr = p_kv[:, KV_LORA:KV_LORA + QK_ROPE]
    kr_sw = p_kv[:, KV_LORA + QK_ROPE:KV_LORA + 2 * QK_ROPE]
    rs = lax.rsqrt(jnp.mean(kr * kr, axis=-1, keepdims=True) + NORM_EPS)
    if cos is None:
        k_rope = kr * rs * gkr
    else:
        k_rope = ((kr * gkr) * cos + (kr_sw * gkr_sw) * sin) * rs
    k_rope = k_rope.astype(BF16)
    kv = jnp.dot(_head_rms(kv_a, kvn).astype(BF16), wkvb, preferred_element_type=F32)
    for h in range(MLA_HEADS):
        kh = kv[:, h * QK_NOPE:(h + 1) * QK_NOPE]
        k_ref[0, h, :, 0:QK_NOPE] = (kh * _lane_rms_scale(kh, ones, QK_NOPE) * gkn).astype(BF16)
        k_ref[0, h, :, QK_NOPE:QK_DIM] = k_rope
        v_ref[0, h, :, 0:V_HEAD] = kv[:, MLA_WIDTH + h * V_HEAD:MLA_WIDTH + (h + 1) * V_HEAD].astype(BF16)
        v_ref[0, h, :, V_HEAD:V_COLS] = jnp.ones((kv.shape[0], V_HEAD), BF16)


def _od_lat_kernel(x_ref, sc_ref, sh_ref, nw_ref, win_ref, qan_ref, kvn_ref, wqb_ref, wkvb_ref,
                   gqn_ref, gqr_ref, gqrs_ref, gkn_ref, gkr_ref, gkrs_ref, ones_ref, e2_ref, cos_ref, sin_ref,
                   kc_ref, vc_ref, q_ref, k_ref, v_ref, z_ref, *, n_ctx_tiles):
    i = pl.program_id(1)

    @pl.when(i < n_ctx_tiles)
    def _():
        k_ref[...] = kc_ref[...]
        v_ref[...] = vc_ref[...]

    @pl.when(i >= n_ctx_tiles)
    def _():
        _od_lat_tile(x_ref, sc_ref, sh_ref, nw_ref, win_ref, qan_ref, kvn_ref, wqb_ref, wkvb_ref,
                     gqn_ref, gqr_ref, gqrs_ref, gkn_ref, gkr_ref, gkrs_ref, ones_ref, e2_ref, cos_ref, sin_ref,
                     q_ref, k_ref, v_ref, z_ref)


def _od_lat_tile(x_ref, sc_ref, sh_ref, nw_ref, win_ref, qan_ref, kvn_ref, wqb_ref, wkvb_ref,
                 gqn_ref, gqr_ref, gqrs_ref, gkn_ref, gkr_ref, gkrs_ref, ones_ref, e2_ref, cos_ref, sin_ref,
                 q_ref, k_ref, v_ref, z_ref):
    h = _modulated_norm(x_ref[0], nw_ref[...], sc_ref[0], sh_ref[0]).astype(BF16)
    p = jnp.dot(h, win_ref[...], preferred_element_type=F32)
    kv_off = Q_LORA
    z_off = Q_LORA + KV_LORA + 2 * QK_ROPE
    z_ref[0] = p[:, z_off:].astype(BF16)
    cos = cos_ref[...]
    sin = sin_ref[...]
    _keys_values(p[:, kv_off:z_off], kvn_ref[...], wkvb_ref[...], gkn_ref[...], gkr_ref[...], gkrs_ref[...],
                 cos[:, :QK_ROPE], sin[:, :QK_ROPE], ones_ref[...], k_ref, v_ref)

    q = jnp.dot(_head_rms(p[:, :Q_LORA], qan_ref[...]).astype(BF16), wqb_ref[...],
                preferred_element_type=F32)
    rw = MLA_HEADS * QK_ROPE
    qr = q[:, MLA_WIDTH:MLA_WIDTH + rw]
    qr_sw = q[:, MLA_WIDTH + rw:MLA_WIDTH + 2 * rw]
    rot = (qr * gqr_ref[...]) * cos + (qr_sw * gqrs_ref[...]) * sin
    ones = ones_ref[...]
    e2 = e2_ref[...]
    gqn = gqn_ref[...] * Q_SCALE
    rot = jnp.concatenate(
        [rot[:, j * LANES:(j + 1) * LANES] * (_lane_rms_scale(qr[:, j * LANES:(j + 1) * LANES], e2, QK_ROPE) * Q_SCALE)
         for j in range(rw // LANES)], axis=1).astype(BF16)
    for hd in range(MLA_HEADS):
        qh = q[:, hd * QK_NOPE:(hd + 1) * QK_NOPE]
        q_ref[0, hd, :, 0:QK_NOPE] = (qh * _lane_rms_scale(qh, ones, QK_NOPE) * gqn).astype(BF16)
        q_ref[0, hd, :, QK_NOPE:QK_DIM] = rot[:, hd * QK_ROPE:(hd + 1) * QK_ROPE]


def _od_ctx_kernel(x_ref, sc_ref, sh_ref, nw_ref, win_ref, kvn_ref, wkvb_ref,
                   gkn_ref, gkr_ref, gkrs_ref, ones_ref, k_ref, v_ref):
    h = _modulated_norm(x_ref[0], nw_ref[...], sc_ref[0], sh_ref[0]).astype(BF16)
    p = jnp.dot(h, win_ref[...], preferred_element_type=F32)
    _keys_values(p, kvn_ref[...], wkvb_ref[...], gkn_ref[...], gkr_ref[...], gkrs_ref[...],
                 None, None, ones_ref[...], k_ref, v_ref)


def _od_ctx_call(ctx, scale, shift, nw, win_kv, kvn, wkvb, gkn, gkr, gkrs, ones, tm):
    nb, tc, d = ctx.shape
    row = lambda b, i: (b, i, 0)
    vec = lambda b, i: (b, 0, 0)
    head = lambda b, i: (b, 0, i, 0)
    params = [nw, win_kv, kvn, wkvb, gkn, gkr, gkrs, ones]
    return pl.pallas_call(
        _od_ctx_kernel,
        grid=(nb, tc // tm),
        in_specs=[pl.BlockSpec((1, tm, d), row), pl.BlockSpec((1, 1, d), vec), pl.BlockSpec((1, 1, d), vec)]
                 + [_full(p.shape) for p in params],
        out_specs=[pl.BlockSpec((1, MLA_HEADS, tm, QK_DIM), head),
                   pl.BlockSpec((1, MLA_HEADS, tm, V_COLS), head)],
        out_shape=[jax.ShapeDtypeStruct((nb, MLA_HEADS, tc, QK_DIM), BF16),
                   jax.ShapeDtypeStruct((nb, MLA_HEADS, tc, V_COLS), BF16)],
        compiler_params=_cparams("parallel", "parallel"),
        name="od_in_ctx",
    )(ctx, scale, shift, *params)


def _od_lat_call(x, scale, shift, params, cos, sin, k_ctx, v_ctx, tm):
    nb, t, d = x.shape
    tc = k_ctx.shape[2]
    assert tc % tm == 0
    n_off = tc // tm
    lat = lambda i: jnp.maximum(i - n_off, 0)
    cxt = lambda i: jnp.minimum(i, n_off - 1)
    row = lambda b, i: (b, lat(i), 0)
    vec = lambda b, i: (b, 0, 0)
    head_q = lambda b, i: (b, 0, lat(i), 0)
    head_c = lambda b, i: (b, 0, cxt(i), 0)
    head_k = lambda b, i: (b, 0, i, 0)
    tab = lambda b, i: (lat(i), 0)
    rw = MLA_HEADS * QK_ROPE
    return pl.pallas_call(
        functools.partial(_od_lat_kernel, n_ctx_tiles=n_off),
        grid=(nb, n_off + t // tm),
        in_specs=[pl.BlockSpec((1, tm, d), row), pl.BlockSpec((1, 1, d), vec), pl.BlockSpec((1, 1, d), vec)]
                 + [_full(p.shape) for p in params]
                 + [pl.BlockSpec((tm, rw), tab), pl.BlockSpec((tm, rw), tab),
                    pl.BlockSpec((1, MLA_HEADS, tm, QK_DIM), head_c),
                    pl.BlockSpec((1, MLA_HEADS, tm, V_COLS), head_c)],
        out_specs=[pl.BlockSpec((1, MLA_HEADS, tm, QK_DIM), head_q),
                   pl.BlockSpec((1, MLA_HEADS, tm, QK_DIM), head_k),
                   pl.BlockSpec((1, MLA_HEADS, tm, V_COLS), head_k),
                   pl.BlockSpec((1, tm, MLA_WIDTH), row)],
        out_shape=[jax.ShapeDtypeStruct((nb, MLA_HEADS, t, QK_DIM), BF16),
                   jax.ShapeDtypeStruct((nb, MLA_HEADS, tc + t, QK_DIM), BF16),
                   jax.ShapeDtypeStruct((nb, MLA_HEADS, tc + t, V_COLS), BF16),
                   jax.ShapeDtypeStruct((nb, t, MLA_WIDTH), BF16)],
        compiler_params=_cparams("parallel", "arbitrary"),
        name="od_in_lat",
    )(x, scale, shift, *params, cos, sin, k_ctx, v_ctx)


ATTN_SUB = 512


def _attn_kernel(q_ref, k_ref, v_ref, o_ref):
    k = k_ref[0, 0]
    v = v_ref[0, 0]
    sub = min(ATTN_SUB, q_ref.shape[2])
    n_sub = q_ref.shape[2] // sub

    def scores(i):
        return lax.dot_general(q_ref[0, 0, i * sub:(i + 1) * sub, :], k, (((1,), (1,)), ((), ())),
                               preferred_element_type=F32)

    s = scores(0)
    for i in range(n_sub):
        s_next = scores(i + 1) if i + 1 < n_sub else None
        p = jnp.exp2((s - jnp.max(s, axis=-1, keepdims=True)).astype(BF16))
        ov = jnp.dot(p, v, preferred_element_type=F32)
        o_ref[0, i * sub:(i + 1) * sub, :] = (ov[:, :V_HEAD] / ov[:, V_HEAD:]).astype(BF16)
        s = s_next


def _attn_call(q, k, v, tq):
    nb, nh, t, _ = q.shape
    tk = k.shape[2]
    whole = lambda b, h, i: (b, h, 0, 0)
    return pl.pallas_call(
        _attn_kernel,
        grid=(nb, nh, t // tq),
        in_specs=[pl.BlockSpec((1, 1, tq, QK_DIM), lambda b, h, i: (b, h, i, 0)),
                  pl.BlockSpec((1, 1, tk, QK_DIM), whole),
                  pl.BlockSpec((1, 1, tk, V_COLS), whole)],
        out_specs=pl.BlockSpec((1, tq, V_HEAD), lambda b, h, i: (b, i, h)),
        out_shape=jax.ShapeDtypeStruct((nb, t, nh * V_HEAD), BF16),
        compiler_params=_cparams("parallel", "parallel", "parallel"),
        name="mla_attn",
    )(q, k, v)


def _od_out_kernel(o_ref, z_ref, wo_ref, x_ref, g_ref, out_ref):
    a = (o_ref[0].astype(F32) * _silu(z_ref[0].astype(F32))).astype(BF16)
    out_ref[0] = x_ref[0] + g_ref[0] * jnp.dot(a, wo_ref[...], preferred_element_type=F32)


def _od_out_call(o, z, wo_bf16, x, gate, tm):
    nb, t, d = x.shape
    row = lambda b, i: (b, i, 0)
    return pl.pallas_call(
        _od_out_kernel,
        grid=(nb, t // tm),
        in_specs=[pl.BlockSpec((1, tm, MLA_WIDTH), row), pl.BlockSpec((1, tm, MLA_WIDTH), row),
                  _full(wo_bf16.shape), pl.BlockSpec((1, tm, d), row),
                  pl.BlockSpec((1, 1, d), lambda b, i: (b, 0, 0))],
        out_specs=pl.BlockSpec((1, tm, d), row),
        out_shape=jax.ShapeDtypeStruct((nb, t, d), F32),
        compiler_params=_cparams("parallel", "parallel"),
        name="od_out",
    )(o, z, wo_bf16, x, gate)


def _rope_tables(n):
    rows = n // GRID_W
    row = jnp.repeat(jnp.arange(rows, dtype=F32), GRID_W)
    col = jnp.tile(jnp.arange(GRID_W, dtype=F32), rows)
    inv_freq = ROPE_THETA ** (-jnp.arange(ROPE_PAIRS, dtype=F32) / ROPE_PAIRS)
    ar = row[:, None] * inv_freq
    ac = col[:, None] * inv_freq
    cos = jnp.concatenate([jnp.cos(ar), jnp.cos(ar), jnp.cos(ac), jnp.cos(ac)], axis=1)
    sin = jnp.concatenate([-jnp.sin(ar), jnp.sin(ar), -jnp.sin(ac), jnp.sin(ac)], axis=1)
    return cos, sin


def _row_tile(t, want):
    tm = min(t, want)
    assert t % tm == 0 and tm % HALO == 0
    return tm


def kernel(x, c, ctx, c_ctx, ada_w, ada_b, norm_w, ev_w_in, ev_conv_w, ev_mu, ev_k_k, ev_k_a, ev_w0, ev_w2, ev_a0, ev_a2, ev_r_k, ev_lnx_w, ev_lnx_b, ev_w_out, od_w_in, od_q_a_norm, od_kv_a_norm, od_w_qb, od_w_kvb, od_gq_nope, od_gq_rope, od_gk_nope, od_gk_rope, od_w_o):
    nb, t, d = x.shape
    tc = ctx.shape[1]
    assert t % CHUNK == 0 and tc % CHUNK == 0 and t % GRID_W == 0

    rows = -(-(nb + 1) // HALO) * HALO
    cc = jnp.zeros((rows, d), F32).at[:nb].set(c).at[nb].set(c_ctx)
    mod = _ada_call(cc, ada_w, ada_b)

    def mods(layer):
        m = mod[layer]
        lat = [m[:nb, i * d:(i + 1) * d].reshape(nb, 1, d) for i in range(3)]
        cx = [jnp.broadcast_to(m[nb, i * d:(i + 1) * d].reshape(1, 1, d), (nb, 1, d)) for i in range(3)]
        return lat, cx

    e_np = np.arange(PAIR)[:, None] // RWKV_HEAD == np.arange(PAIR)[None, :] // RWKV_HEAD
    e_bf16 = jnp.asarray(e_np, BF16)
    ones_bf16 = jnp.ones((LANES, LANES), BF16)

    (sh_l, sc_l, g_l), (sh_c, sc_c, g_c) = mods(0)
    nw0 = norm_w[0].reshape(1, d)
    w_in0 = ev_w_in[0].astype(BF16)
    tm_l = _row_tile(t, 256)
    tm_c = _row_tile(tc, 256)
    pa_c, pr_c, zb_c = _ev_in_call(ctx, sc_c, sh_c, nw0, w_in0, tm_c)
    pa_l, pr_l, zb_l = _ev_in_call(x, sc_l, sh_l, nw0, w_in0, tm_l)

    W = RWKV_WIDTH
    vec = lambda a: a.reshape(1, -1)
    lora = jnp.zeros((2, 2 * LORA, 2 * W), F32)
    lora = lora.at[:, :LORA, :W].set(ev_w2[0]).at[:, LORA:, W:].set(ev_a2[0])
    lora_hi = lora.astype(BF16)
    lora_lo = (lora - lora_hi.astype(F32)).astype(BF16)
    wa0 = jnp.concatenate([ev_w0[0], ev_a0[0]], axis=-1).reshape(2, 1, 2 * W)
    params = [vec(1.0 - ev_mu[0]), vec(0.5 * ev_mu[0]), vec(ev_k_k[0]), vec(ev_k_a[0]), vec(1.0 - ev_k_a[0]),
              vec(ev_r_k[0]), wa0, lora_hi, lora_lo, e_bf16]
    yf_c, yb_c, bf_c, bb_c, s_c = _scan_call(pr_c, params, None)
    yf_l, yb_l, bf_l, bb_l, _ = _scan_call(pr_l, params, s_c)

    wo0 = ev_w_out[0].astype(BF16)
    lw, lb = vec(ev_lnx_w[0]), vec(ev_lnx_b[0])
    x1 = _ev_out_call(pa_l, zb_l, yf_l, yb_l, bf_l, bb_l, ev_conv_w[0], lw, lb, e_bf16, wo0, x, g_l,
                      _row_tile(t, 512))
    ctx1 = _ev_out_call(pa_c, zb_c, yf_c, yb_c, bf_c, bb_c, ev_conv_w[0], lw, lb, e_bf16, wo0, ctx, g_c, tm_c)

    (sh_l, sc_l, g_l), (sh_c, sc_c, _) = mods(1)
    nw1 = norm_w[1].reshape(1, d)
    swap = np.arange(QK_ROPE) ^ ROPE_PAIRS
    hq = np.arange(MLA_HEADS)[:, None] * QK_DIM
    idx_q = np.concatenate([(hq + np.arange(QK_NOPE)).ravel(),
                            (hq + QK_NOPE + np.arange(QK_ROPE)).ravel(),
                            (hq + QK_NOPE + swap).ravel()])
    hk = np.arange(MLA_HEADS)[:, None] * (QK_NOPE + V_HEAD)
    idx_kv = np.concatenate([(hk + np.arange(QK_NOPE)).ravel(), (hk + QK_NOPE + np.arange(V_HEAD)).ravel()])
    r_off = Q_LORA + KV_LORA
    idx_in = np.concatenate([np.arange(r_off + QK_ROPE), r_off + swap,
                             np.arange(r_off + QK_ROPE, od_w_in.shape[2])])
    w_in1 = od_w_in[0][:, idx_in].astype(BF16)
    wqb = od_w_qb[0][:, idx_q].astype(BF16)
    wkvb = od_w_kvb[0][:, idx_kv].astype(BF16)
    gqr = jnp.tile(od_gq_rope[0], MLA_HEADS).reshape(1, -1)
    gqrs = jnp.tile(od_gq_rope[0][swap], MLA_HEADS).reshape(1, -1)
    gkr, gkrs = vec(od_gk_rope[0]), vec(od_gk_rope[0][swap])
    gqn, gkn = vec(od_gq_nope[0]), vec(od_gk_nope[0])
    qan, kvn = vec(od_q_a_norm[0]), vec(od_kv_a_norm[0])
    cos, sin = _rope_tables(t)
    cos = jnp.tile(cos, (1, MLA_HEADS))
    sin = jnp.tile(sin, (1, MLA_HEADS))

    k_c, v_c = _od_ctx_call(ctx1, sc_c, sh_c, nw1, w_in1[:, Q_LORA:Q_LORA + KV_LORA + 2 * QK_ROPE],
                            kvn, wkvb, gkn, gkr, gkrs, ones_bf16, _row_tile(tc, 256))
    params = [nw1, w_in1, qan, kvn, wqb, wkvb, gqn, gqr, gqrs, gkn, gkr, gkrs, ones_bf16, e_bf16]
    q, k_all, v_all, z = _od_lat_call(x1, sc_l, sh_l, params, cos, sin, k_c, v_c, _row_tile(tc, 256))
    o = _attn_call(q, k_all, v_all, _row_tile(t, 2 * ATTN_SUB))
    return _od_out_call(o, z, od_w_o[0].astype(BF16), x1, g_l, _row_tile(t, 512))
```

```python
import functools

import numpy as np
import jax
import jax.numpy as jnp
from jax import lax
from jax.experimental import pallas as pl
from jax.experimental.pallas import tpu as pltpu

F32 = jnp.float32
BF16 = jnp.bfloat16

NORM_EPS = 1e-6
GRID_W = 64

CONV_WIDTH = 512
RWKV_HEAD = 64
RWKV_WIDTH = 512
LORA = 64
RWKV_GN_EPS = 64e-5
DECAY_SCALE = -float(np.exp(-0.5))
A_COLS = 4 * CONV_WIDTH
SHIFT_COLS = 3 * RWKV_WIDTH + 2 * LORA
CHUNK = 64
PAIR = 2 * RWKV_HEAD
N_PAIRS = RWKV_WIDTH // PAIR

MLA_HEADS = 8
QK_NOPE = 128
QK_ROPE = 64
QK_DIM = QK_NOPE + QK_ROPE
V_HEAD = 128
Q_LORA = 384
KV_LORA = 256
MLA_WIDTH = MLA_HEADS * V_HEAD
SM_SCALE = QK_DIM ** -0.5
LOG2E = 1.4426950408889634
Q_SCALE = SM_SCALE * LOG2E
V_COLS = 2 * V_HEAD
ROPE_THETA = 10000.0
ROPE_PAIRS = QK_ROPE // 4
LANES = 128

VMEM_LIMIT_BYTES = 56 * 1024 * 1024
HALO = 8
HALO_BF16 = 16


def _cparams(*sem):
    return pltpu.CompilerParams(dimension_semantics=sem, vmem_limit_bytes=VMEM_LIMIT_BYTES)


def _silu(x):
    return x / (1.0 + jnp.exp(-x))


def _dot(a, b):
    return jnp.dot(a.astype(BF16), b.astype(BF16), preferred_element_type=F32)


def _dot_nt(a, b):
    return lax.dot_general(a.astype(BF16), b.astype(BF16), (((1,), (1,)), ((), ())),
                           preferred_element_type=F32)


def _bf16_split(x):
    hi = x.astype(BF16)
    return hi, (x - hi.astype(F32)).astype(BF16)


def _mm(a, b):
    return jnp.dot(a, b, preferred_element_type=F32)


def _dot3(a, b_hi, b_lo):
    a_hi, a_lo = _bf16_split(a)
    return _mm(a_hi, b_hi) + (_mm(a_lo, b_hi) + _mm(a_hi, b_lo))


def _exact_lhs_dot(m_bf16, x):
    hi, lo = _bf16_split(x)
    return _mm(m_bf16, hi) + _mm(m_bf16, lo)


def _segsum(x, e2, passes=2):
    out = []
    for i in range(x.shape[1] // PAIR):
        hi, lo = _bf16_split(x[:, i * PAIR:(i + 1) * PAIR])
        out.append(_mm(hi, e2) + _mm(lo, e2) if passes == 2 else _mm(hi, e2))
    return jnp.concatenate(out, axis=1)


def _full(shape):
    return pl.BlockSpec(shape, lambda *_: (0,) * len(shape))


def _ada_kernel(c_ref, w_ref, b_ref, o_ref):
    o_ref[0] = _dot3(_silu(c_ref[...]), *_bf16_split(w_ref[0])) + b_ref[0]


def _ada_call(cc, ada_w, ada_b):
    depth, d, d3 = ada_w.shape
    rows = cc.shape[0]
    tn = 1024
    return pl.pallas_call(
        _ada_kernel,
        grid=(depth, d3 // tn),
        in_specs=[pl.BlockSpec((rows, d), lambda l, j: (0, 0)),
                  pl.BlockSpec((1, d, tn), lambda l, j: (l, 0, j)),
                  pl.BlockSpec((1, 1, tn), lambda l, j: (l, 0, j))],
        out_specs=pl.BlockSpec((1, rows, tn), lambda l, j: (l, 0, j)),
        out_shape=jax.ShapeDtypeStruct((depth, rows, d3), F32),
        compiler_params=_cparams("arbitrary", "arbitrary"),
        name="ada_mod",
    )(cc, ada_w, ada_b.reshape(depth, 1, d3))


def _modulated_norm(x, nw, scale, shift):
    rs = lax.rsqrt(jnp.mean(x * x, axis=-1, keepdims=True) + NORM_EPS)
    return (x * rs * nw) * (1.0 + scale) + shift


def _ev_in_kernel(x_ref, sc_ref, sh_ref, nw_ref, w_ref, pa_ref, pr_ref, zb_ref):
    h = _modulated_norm(x_ref[0], nw_ref[...], sc_ref[0], sh_ref[0]).astype(BF16)
    pa_ref[0] = jnp.dot(h, w_ref[:, :A_COLS], preferred_element_type=F32).astype(BF16)
    pr_ref[0] = jnp.dot(h, w_ref[:, A_COLS:A_COLS + SHIFT_COLS], preferred_element_type=F32)
    zb_ref[0] = jnp.dot(h, w_ref[:, A_COLS + SHIFT_COLS:], preferred_element_type=F32).astype(BF16)


def _ev_in_call(x, scale, shift, nw, w_bf16, tm):
    nb, t, d = x.shape
    ncol = w_bf16.shape[1]
    row = lambda b, i: (b, i, 0)
    vec = lambda b, i: (b, 0, 0)
    return pl.pallas_call(
        _ev_in_kernel,
        grid=(nb, t // tm),
        in_specs=[pl.BlockSpec((1, tm, d), row),
                  pl.BlockSpec((1, 1, d), vec),
                  pl.BlockSpec((1, 1, d), vec),
                  _full((1, d)),
                  _full((d, ncol))],
        out_specs=[pl.BlockSpec((1, tm, A_COLS), row),
                   pl.BlockSpec((1, tm, SHIFT_COLS), row),
                   pl.BlockSpec((1, tm, RWKV_WIDTH), row)],
        out_shape=[jax.ShapeDtypeStruct((nb, t, A_COLS), BF16),
                   jax.ShapeDtypeStruct((nb, t, SHIFT_COLS), F32),
                   jax.ShapeDtypeStruct((nb, t, RWKV_WIDTH), BF16)],
        compiler_params=_cparams("parallel", "parallel"),
        name="ev_in",
    )(x, scale, shift, nw, w_bf16)


def _neighbours(cur, prev_halo, next_halo, is_first, is_last):
    n = cur.shape[0]
    rowi = lax.broadcasted_iota(jnp.int32, cur.shape, 0)
    prev_row = jnp.where(is_first, 0.0, prev_halo[prev_halo.shape[0] - 1:, :])
    next_row = jnp.where(is_last, 0.0, next_halo[0:1, :])
    prev = jnp.where(rowi == 0, prev_row, pltpu.roll(cur, 1, 0))
    nxt = jnp.where(rowi == n - 1, next_row, pltpu.roll(cur, n - 1, 0))
    return prev, nxt


OPERANDS = ("rh", "ah", "bc", "kc", "bt", "kt", "v", "bon")


def _scan_prep_pieces(reverse, cur_ref, prv_ref, nxt_ref, is_first, is_last, prm, w_dir, ops_ref, gc_ref, bi, d):
    C = CHUNK
    W = RWKV_WIDTH
    mu_keep, mu_half, k_k, k_a, ka_keep, r_k, e2 = prm
    wa0, lora_hi, lora_lo = w_dir
    t = {}

    def shift():
        pr = cur_ref[bi]
        prev, nxt = _neighbours(pr, prv_ref[bi], nxt_ref[bi], is_first, is_last)
        x = mu_keep * pr + mu_half * (prev + nxt)
        t.update(r=x[:, 0:W], k=x[:, W:2 * W], v=x[:, 2 * W:3 * W], wa=x[:, 3 * W:3 * W + 2 * LORA])

    def rates():
        lane = lax.broadcasted_iota(jnp.int32, (C, PAIR), 1)
        tw = jnp.where(lane < LORA, jnp.tanh(t["wa"]), t["wa"])
        lo = _dot3(tw, lora_hi, lora_lo)
        sig = 1.0 / (1.0 + jnp.exp(-(wa0 + lo)))
        t.update(lw=DECAY_SCALE * sig[:, :W], iclr=sig[:, W:])

    def keys():
        k, iclr = t["k"], t["iclr"]
        kkv = k * k_k
        kkn = kkv * lax.rsqrt(jnp.maximum(_segsum(kkv * kkv, e2, passes=1), 1e-24))
        k_dir = k * (ka_keep + iclr * k_a)
        t.update(kkn=kkn, k_dir=k_dir, bvec=kkn * iclr)
        ops_ref[bi, d, OPERANDS.index("bon")] = _segsum(t["r"] * k_dir * r_k, e2) * t["v"]

    def decays():
        lw = t["lw"]
        ti = lax.broadcasted_iota(jnp.int32, (C, C), 0)
        si = lax.broadcasted_iota(jnp.int32, (C, C), 1)
        tri = ((si >= ti) if reverse else (si <= ti)).astype(BF16)
        g_in = _exact_lhs_dot(tri, lw)
        g_tot = jnp.sum(lw, axis=0, keepdims=True)
        t.update(gam=jnp.exp(g_in), gam_ex=jnp.exp(g_in - lw), gam_inv=jnp.exp(-g_in), gam_c=jnp.exp(g_tot))

    def store():
        gam_c = t["gam_c"]
        bc = t["bvec"] * t["gam_inv"]
        kc = t["k_dir"] * t["gam_inv"]
        out = dict(rh=t["r"] * t["gam"], ah=-t["kkn"] * t["gam_ex"], bc=bc, kc=kc,
                   bt=bc * gam_c, kt=kc * gam_c, v=t["v"])
        for name, val in out.items():
            ops_ref[bi, d, OPERANDS.index(name)] = val
        gc_ref[bi, d] = gam_c

    return [shift, rates, keys, decays, store]


def _scan_kernel(n_chunks, has_s0, n_bb, *refs):
    refs = list(refs)
    first_refs = (refs[0:3], refs[3:6])
    next_refs = (refs[6:9], refs[9:12])
    (muk_ref, muh_ref, kk_ref, ka_ref, kak_ref, rk_ref, wa0_ref, lh_ref, ll_ref, e_ref) = refs[12:22]
    pos = 22
    s0_ref = None
    if has_s0:
        s0_ref = refs[pos]
        pos += 1
    y_refs = refs[pos:pos + 2]
    bon_refs = refs[pos + 2:pos + 4]
    sfin_ref, st_ref, ops_a, gc_a, ops_b, gc_b = refs[pos + 4:pos + 10]

    C = CHUNK
    c = pl.program_id(1)
    prm = (muk_ref[...], muh_ref[...], kk_ref[...], ka_ref[...], kak_ref[...], rk_ref[...], e_ref[...])

    def pieces(d, src, is_first, is_last, ops_dst, gc_dst):
        cur, prv, nxt = src
        per_b = [_scan_prep_pieces(d == 1, cur, prv, nxt, is_first, is_last, prm,
                                   (wa0_ref[d], lh_ref[d], ll_ref[d]), ops_dst, gc_dst, bi, d) for bi in range(n_bb)]
        return [fn for group in zip(*per_b) for fn in group]

    @pl.when(c == 0)
    def _():
        if has_s0:
            st_ref[...] = s0_ref[...]
        else:
            st_ref[...] = jnp.zeros_like(st_ref)
        for piece in pieces(0, first_refs[0], True, n_chunks == 1, ops_a, gc_a) + \
                pieces(1, first_refs[1], n_chunks == 1, True, ops_a, gc_a):
            piece()

    nxt_f = jnp.minimum(c + 1, n_chunks - 1)
    nxt_b = jnp.maximum(n_chunks - 2 - c, 0)

    def chain(ops_ref, gc_ref, overlap):
        n2 = 2 * C
        ri = lax.broadcasted_iota(jnp.int32, (n2, n2), 0)
        ci = lax.broadcasted_iota(jnp.int32, (n2, n2), 1)
        same_head = (ri >= C) == (ci >= C)
        rt = ri & (C - 1)
        ct = ci & (C - 1)
        strict = (same_head & (ct < rt), same_head & (ct > rt))
        incl = (same_head & (ct <= rt), same_head & (ct >= rt))
        diag = ri == ci
        eye = diag.astype(F32)
        blocks = [same_head & ((rt >> s) == (ct >> s)) for s in range(1, C.bit_length())]
        first_head = lax.broadcasted_iota(jnp.int32, (C, PAIR), 1) < RWKV_HEAD

        def stk(z):
            return jnp.concatenate([jnp.where(first_head, z, 0.0), jnp.where(first_head, 0.0, z)], axis=0)

        def dup(z):
            return jnp.concatenate([z, z], axis=0)

        chains = [(bi, d, slice(p * PAIR, (p + 1) * PAIR), p)
                  for bi in range(n_bb) for d in range(2) for p in range(N_PAIRS)]
        n_ch = len(chains)

        def operand(name, bi, d, sl):
            return ops_ref[bi, d, OPERANDS.index(name), :, sl]

        ah_s = [stk(operand("ah", bi, d, sl)) for bi, d, sl, _ in chains]
        rh_s = [stk(operand("rh", bi, d, sl)) for bi, d, sl, _ in chains]
        v_s = [stk(operand("v", bi, d, sl)) for bi, d, sl, _ in chains]
        pm = [_dot_nt(jnp.concatenate([ah_s[i], rh_s[i]], axis=0),
                      jnp.concatenate([dup(operand("bc", bi, d, sl)), dup(operand("kc", bi, d, sl))], axis=0))
              for i, (bi, d, sl, _) in enumerate(chains)]
        overlap()
        a_ab = [jnp.where(strict[d], pm[i][:n2, :n2], 0.0) for i, (_, d, _, _) in enumerate(chains)]
        a_ak = [jnp.where(strict[d], pm[i][:n2, n2:], 0.0) for i, (_, d, _, _) in enumerate(chains)]
        a_r = [jnp.concatenate([jnp.where(incl[d], pm[i][n2:, :n2], 0.0),
                                jnp.where(incl[d], pm[i][n2:, n2:], 0.0)], axis=1)
               for i, (_, d, _, _) in enumerate(chains)]
        akv = [_dot(a_ak[i], v_s[i]) for i in range(n_ch)]
        tinv = [jnp.where(blocks[0], a, 0.0) + eye for a in a_ab]
        overlap()
        for lvl in range(1, len(blocks)):
            off = blocks[lvl] & ~blocks[lvl - 1]
            s = 1 << lvl
            if s < HALO:
                t1 = [_dot(jnp.where(off, a_ab[i], 0.0), tinv[i]) for i in range(n_ch)]
                overlap()
                t2 = [_dot(tinv[i], t1[i]) for i in range(n_ch)]
                tinv = [tinv[i] + t2[i] for i in range(n_ch)]
            else:
                def band(x, d):
                    return jnp.concatenate([x[j * s:(j + 1) * s] for j in range(1 - d, n2 // s, 2)], axis=0)

                def unband(x, d, rest):
                    return jnp.concatenate([x[(j // 2) * s:(j // 2 + 1) * s] if j % 2 == 1 - d
                                            else rest[j * s:(j + 1) * s] for j in range(n2 // s)], axis=0)

                t1 = [_dot(band(jnp.where(off, a_ab[i], 0.0), d), tinv[i]) for i, (_, d, _, _) in enumerate(chains)]
                overlap()
                zero = jnp.zeros((n2, n2), F32)
                t2 = [_dot(band(tinv[i], d), unband(t1[i], d, zero)) for i, (_, d, _, _) in enumerate(chains)]
                tinv = [unband(band(tinv[i], d) + t2[i], d, tinv[i]) for i, (_, d, _, _) in enumerate(chains)]
            overlap()
        xx = [_dot(tinv[i], jnp.concatenate([ah_s[i], akv[i]], axis=1)) for i in range(n_ch)]
        zeros = jnp.zeros((n2, PAIR), F32)
        rr = []
        for i, (bi, d, sl, _) in enumerate(chains):
            rhs = jnp.concatenate([xx[i], jnp.concatenate([zeros, v_s[i]], axis=1)], axis=0)
            lhs = jnp.concatenate(
                [a_r[i], jnp.concatenate([stk(operand("bt", bi, d, sl)).T, stk(operand("kt", bi, d, sl)).T], axis=1)],
                axis=0)
            rr.append(_dot(lhs, rhs))
        for i, (bi, d, sl, p) in enumerate(chains):
            q_t = rh_s[i] + rr[i][:n2, :PAIR]
            st = st_ref[bi, d, p]
            st_hi, st_lo = _bf16_split(st)
            m_off = rr[i][n2:, :PAIR].astype(BF16)
            qm = _mm(jnp.concatenate([q_t.astype(BF16), m_off], axis=0), st_hi)
            ys = qm[:n2] + rr[i][:n2, PAIR:]
            g_col = jnp.sum(jnp.where(diag, gc_ref[bi, d][:, sl], 0.0), axis=1, keepdims=True)
            st_ref[bi, d, p] = (g_col * st + (qm[n2:] + _mm(m_off, st_lo))) + rr[i][n2:, PAIR:]
            y_refs[d][bi, :, sl] = ys[:C] + ys[C:]
        for bi in range(n_bb):
            for d in range(2):
                bon_refs[d][bi] = ops_ref[bi, d, OPERANDS.index("bon")]

    def step(ops_ref, gc_ref, ops_nxt, gc_nxt):
        prep_f = pieces(0, next_refs[0], nxt_f == 0, nxt_f == n_chunks - 1, ops_nxt, gc_nxt)
        prep_b = pieces(1, next_refs[1], nxt_b == 0, nxt_b == n_chunks - 1, ops_nxt, gc_nxt)
        background = [fn for pair in zip(prep_f, prep_b) for fn in pair]

        per_call = -(-len(background) // 12)

        def overlap():
            for _ in range(per_call):
                if background:
                    background.pop(0)()

        chain(ops_ref, gc_ref, overlap)
        while background:
            overlap()

    @pl.when(c % 2 == 0)
    def _():
        step(ops_a, gc_a, ops_b, gc_b)

    @pl.when(c % 2 == 1)
    def _():
        step(ops_b, gc_b, ops_a, gc_a)

    @pl.when(c == n_chunks - 1)
    def _():
        sfin_ref[...] = st_ref[...]


SCAN_BATCH_BLOCK = 2


def _scan_call(pr, params, s0):
    nb, t, _ = pr.shape
    bb = SCAN_BATCH_BLOCK if nb % SCAN_BATCH_BLOCK == 0 else 1
    C = CHUNK
    n_chunks = t // C
    hb = C // HALO
    n_hblk = t // HALO

    def specs(cs_of):
        return [pl.BlockSpec((bb, C, SHIFT_COLS), lambda b, c: (b, cs_of(c), 0)),
                pl.BlockSpec((bb, HALO, SHIFT_COLS), lambda b, c: (b, jnp.maximum(cs_of(c) * hb - 1, 0), 0)),
                pl.BlockSpec((bb, HALO, SHIFT_COLS), lambda b, c: (b, jnp.minimum((cs_of(c) + 1) * hb, n_hblk - 1), 0))]

    first_f = lambda c: 0
    first_b = lambda c: n_chunks - 1
    next_f = lambda c: jnp.minimum(c + 1, n_chunks - 1)
    next_b = lambda c: jnp.maximum(n_chunks - 2 - c, 0)
    st_spec = pl.BlockSpec((bb, 2, N_PAIRS, PAIR, PAIR), lambda b, c: (b, 0, 0, 0, 0))
    in_specs = specs(first_f) + specs(first_b) + specs(next_f) + specs(next_b) + [_full(p.shape) for p in params]
    args = [pr] * 12 + list(params)
    if s0 is not None:
        in_specs.append(st_spec)
        args.append(s0)
    out_f = pl.BlockSpec((bb, C, RWKV_WIDTH), lambda b, c: (b, c, 0))
    out_b = pl.BlockSpec((bb, C, RWKV_WIDTH), lambda b, c: (b, n_chunks - 1 - c, 0))
    tok = jax.ShapeDtypeStruct((nb, t, RWKV_WIDTH), F32)
    return pl.pallas_call(
        functools.partial(_scan_kernel, n_chunks, s0 is not None, bb),
        grid=(nb // bb, n_chunks),
        in_specs=in_specs,
        out_specs=[out_f, out_b, out_f, out_b, st_spec],
        out_shape=[tok, tok, tok, tok, jax.ShapeDtypeStruct((nb, 2, N_PAIRS, PAIR, PAIR), F32)],
        scratch_shapes=[pltpu.VMEM((bb, 2, N_PAIRS, PAIR, PAIR), F32),
                        pltpu.VMEM((bb, 2, len(OPERANDS), C, RWKV_WIDTH), F32), pltpu.VMEM((bb, 2, 1, RWKV_WIDTH), F32),
                        pltpu.VMEM((bb, 2, len(OPERANDS), C, RWKV_WIDTH), F32), pltpu.VMEM((bb, 2, 1, RWKV_WIDTH), F32)],
        compiler_params=_cparams("parallel", "arbitrary"),
        name="rwkv_scan",
    )(*args)


def _ev_out_kernel(n_tiles, pa_ref, pap_ref, pan_ref, zb_ref, yf_ref, yb_ref, bf_ref, bb_ref,
                   cw_ref, lw_ref, lb_ref, e_ref, wo_ref, x_ref, g_ref, o_ref):
    i = pl.program_id(1)
    W = CONV_WIDTH
    pa = pa_ref[0].astype(F32)
    u, gate_b, gate_c, z = pa[:, 0:W], pa[:, W:2 * W], pa[:, 2 * W:3 * W], pa[:, 3 * W:4 * W]
    cu = gate_c * u
    pa_p = pap_ref[0].astype(F32)
    pa_n = pan_ref[0].astype(F32)
    cu_p = pa_p[:, 2 * W:3 * W] * pa_p[:, 0:W]
    cu_n = pa_n[:, 2 * W:3 * W] * pa_n[:, 0:W]
    prev, nxt = _neighbours(cu, cu_p, cu_n, i == 0, i == n_tiles - 1)
    cw = cw_ref[...]
    conv = cw[0:1] * prev + cw[1:2] * cu + cw[2:3] * nxt
    a_out = gate_b * conv * _silu(z)

    e = e_ref[...]
    ysum = yf_ref[0] + yb_ref[0]
    inv_n = 1.0 / RWKV_HEAD
    mean = _segsum(ysum, e) * inv_n
    dlt = ysum - mean
    var = _segsum(dlt * dlt, e) * inv_n
    rw = dlt * lax.rsqrt(var + RWKV_GN_EPS) * lw_ref[...] + lb_ref[...] + (bf_ref[0] + bb_ref[0])
    b_out = rw * _silu(zb_ref[0].astype(F32))

    y = (jnp.dot(a_out.astype(BF16), wo_ref[:W, :], preferred_element_type=F32)
         + jnp.dot(b_out.astype(BF16), wo_ref[W:, :], preferred_element_type=F32))
    o_ref[0] = x_ref[0] + g_ref[0] * y


def _ev_out_call(pa, zb, yf, yb, bf, bb, conv_w, lnx_w, lnx_b, e_bf16, wo_bf16, x, gate, tm):
    nb, t, d = x.shape
    n_tiles = t // tm
    hb = tm // HALO_BF16
    n_hblk = t // HALO_BF16
    row = lambda b, i: (b, i, 0)
    prv = lambda b, i: (b, jnp.maximum(i * hb - 1, 0), 0)
    nxt = lambda b, i: (b, jnp.minimum((i + 1) * hb, n_hblk - 1), 0)
    vec = lambda b, i: (b, 0, 0)
    wspec = pl.BlockSpec((1, tm, RWKV_WIDTH), row)
    return pl.pallas_call(
        functools.partial(_ev_out_kernel, n_tiles),
        grid=(nb, n_tiles),
        in_specs=[pl.BlockSpec((1, tm, A_COLS), row),
                  pl.BlockSpec((1, HALO_BF16, A_COLS), prv),
                  pl.BlockSpec((1, HALO_BF16, A_COLS), nxt),
                  wspec, wspec, wspec, wspec, wspec,
                  _full(conv_w.shape), _full(lnx_w.shape), _full(lnx_b.shape),
                  _full(e_bf16.shape), _full(wo_bf16.shape),
                  pl.BlockSpec((1, tm, d), row),
                  pl.BlockSpec((1, 1, d), vec)],
        out_specs=pl.BlockSpec((1, tm, d), row),
        out_shape=jax.ShapeDtypeStruct((nb, t, d), F32),
        compiler_params=_cparams("parallel", "parallel"),
        name="ev_out",
    )(pa, pa, pa, zb, yf, yb, bf, bb, conv_w, lnx_w, lnx_b, e_bf16, wo_bf16, x, gate)


def _head_rms(t, g):
    return t * lax.rsqrt(jnp.mean(t * t, axis=-1, keepdims=True) + NORM_EPS) * g


def _lane_rms_scale(t, ones_bf16, n):
    ss = jnp.dot((t * t).astype(BF16), ones_bf16, preferred_element_type=F32)
    return lax.rsqrt(ss * (1.0 / n) + NORM_EPS)


def _keys_values(p_kv, kvn, wkvb, gkn, gkr, gkr_sw, cos, sin, ones, k_ref, v_ref):
    kv_a = p_kv[:, :KV_LORA]
    kr = p_kv[:, KV_LORA:KV_LORA + QK_ROPE]
    kr_sw = p_kv[:, KV_LORA + QK_ROPE:KV_LORA + 2 * QK_ROPE]
    rs = lax.rsqrt(jnp.mean(kr * kr, axis=-1, keepdims=True) + NORM_EPS)
    if cos is None:
        k_rope = kr * rs * gkr
    else:
        k_rope = ((kr * gkr) * cos + (kr_sw * gkr_sw) * sin) * rs
    k_rope = k_rope.astype(BF16)
    kv = jnp.dot(_head_rms(kv_a, kvn).astype(BF16), wkvb, preferred_element_type=F32)
    for h in range(MLA_HEADS):
        kh = kv[:, h * QK_NOPE:(h + 1) * QK_NOPE]
        k_ref[0, h, :, 0:QK_NOPE] = (kh * _lane_rms_scale(kh, ones, QK_NOPE) * gkn).astype(BF16)
        k_ref[0, h, :, QK_NOPE:QK_DIM] = k_rope
        v_ref[0, h, :, 0:V_HEAD] = kv[:, MLA_WIDTH + h * V_HEAD:MLA_WIDTH + (h + 1) * V_HEAD].astype(BF16)
        v_ref[0, h, :, V_HEAD:V_COLS] = jnp.ones((kv.shape[0], V_HEAD), BF16)


def _od_lat_kernel(x_ref, sc_ref, sh_ref, nw_ref, win_ref, qan_ref, kvn_ref, wqb_ref, wkvb_ref,
                   gqn_ref, gqr_ref, gqrs_ref, gkn_ref, gkr_ref, gkrs_ref, ones_ref, e2_ref, cos_ref, sin_ref,
                   kc_ref, vc_ref, q_ref, k_ref, v_ref, z_ref, *, n_ctx_tiles):
    i = pl.program_id(1)

    @pl.when(i < n_ctx_tiles)
    def _():
        k_ref[...] = kc_ref[...]
        v_ref[...] = vc_ref[...]

    @pl.when(i >= n_ctx_tiles)
    def _():
        _od_lat_tile(x_ref, sc_ref, sh_ref, nw_ref, win_ref, qan_ref, kvn_ref, wqb_ref, wkvb_ref,
                     gqn_ref, gqr_ref, gqrs_ref, gkn_ref, gkr_ref, gkrs_ref, ones_ref, e2_ref, cos_ref, sin_ref,
                     q_ref, k_ref, v_ref, z_ref)


def _od_lat_tile(x_ref, sc_ref, sh_ref, nw_ref, win_ref, qan_ref, kvn_ref, wqb_ref, wkvb_ref,
                 gqn_ref, gqr_ref, gqrs_ref, gkn_ref, gkr_ref, gkrs_ref, ones_ref, e2_ref, cos_ref, sin_ref,
                 q_ref, k_ref, v_ref, z_ref):
    h = _modulated_norm(x_ref[0], nw_ref[...], sc_ref[0], sh_ref[0]).astype(BF16)
    p = jnp.dot(h, win_ref[...], preferred_element_type=F32)
    kv_off = Q_LORA
    z_off = Q_LORA + KV_LORA + 2 * QK_ROPE
    z_ref[0] = p[:, z_off:].astype(BF16)
    cos = cos_ref[...]
    sin = sin_ref[...]
    _keys_values(p[:, kv_off:z_off], kvn_ref[...], wkvb_ref[...], gkn_ref[...], gkr_ref[...], gkrs_ref[...],
                 cos[:, :QK_ROPE], sin[:, :QK_ROPE], ones_ref[...], k_ref, v_ref)

    q = jnp.dot(_head_rms(p[:, :Q_LORA], qan_ref[...]).astype(BF16), wqb_ref[...],
                preferred_element_type=F32)
    rw = MLA_HEADS * QK_ROPE
    qr = q[:, MLA_WIDTH:MLA_WIDTH + rw]
    qr_sw = q[:, MLA_WIDTH + rw:MLA_WIDTH + 2 * rw]
    rot = (qr * gqr_ref[...]) * cos + (qr_sw * gqrs_ref[...]) * sin
    ones = ones_ref[...]
    e2 = e2_ref[...]
    gqn = gqn_ref[...] * Q_SCALE
    rot = jnp.concatenate(
        [rot[:, j * LANES:(j + 1) * LANES] * (_lane_rms_scale(qr[:, j * LANES:(j + 1) * LANES], e2, QK_ROPE) * Q_SCALE)
         for j in range(rw // LANES)], axis=1).astype(BF16)
    for hd in range(MLA_HEADS):
        qh = q[:, hd * QK_NOPE:(hd + 1) * QK_NOPE]
        q_ref[0, hd, :, 0:QK_NOPE] = (qh * _lane_rms_scale(qh, ones, QK_NOPE) * gqn).astype(BF16)
        q_ref[0, hd, :, QK_NOPE:QK_DIM] = rot[:, hd * QK_ROPE:(hd + 1) * QK_ROPE]


def _od_ctx_kernel(x_ref, sc_ref, sh_ref, nw_ref, win_ref, kvn_ref, wkvb_ref,
                   gkn_ref, gkr_ref, gkrs_ref, ones_ref, k_ref, v_ref):
    h = _modulated_norm(x_ref[0], nw_ref[...], sc_ref[0], sh_ref[0]).astype(BF16)
    p = jnp.dot(h, win_ref[...], preferred_element_type=F32)
    _keys_values(p, kvn_ref[...], wkvb_ref[...], gkn_ref[...], gkr_ref[...], gkrs_ref[...],
                 None, None, ones_ref[...], k_ref, v_ref)


def _od_ctx_call(ctx, scale, shift, nw, win_kv, kvn, wkvb, gkn, gkr, gkrs, ones, tm):
    nb, tc, d = ctx.shape
    row = lambda b, i: (b, i, 0)
    vec = lambda b, i: (b, 0, 0)
    head = lambda b, i: (b, 0, i, 0)
    params = [nw, win_kv, kvn, wkvb, gkn, gkr, gkrs, ones]
    return pl.pallas_call(
        _od_ctx_kernel,
        grid=(nb, tc // tm),
        in_specs=[pl.BlockSpec((1, tm, d), row), pl.BlockSpec((1, 1, d), vec), pl.BlockSpec((1, 1, d), vec)]
                 + [_full(p.shape) for p in params],
        out_specs=[pl.BlockSpec((1, MLA_HEADS, tm, QK_DIM), head),
                   pl.BlockSpec((1, MLA_HEADS, tm, V_COLS), head)],
        out_shape=[jax.ShapeDtypeStruct((nb, MLA_HEADS, tc, QK_DIM), BF16),
                   jax.ShapeDtypeStruct((nb, MLA_HEADS, tc, V_COLS), BF16)],
        compiler_params=_cparams("parallel", "parallel"),
        name="od_in_ctx",
    )(ctx, scale, shift, *params)


def _od_lat_call(x, scale, shift, params, cos, sin, k_ctx, v_ctx, tm):
    nb, t, d = x.shape
    tc = k_ctx.shape[2]
    assert tc % tm == 0
    n_off = tc // tm
    lat = lambda i: jnp.maximum(i - n_off, 0)
    cxt = lambda i: jnp.minimum(i, n_off - 1)
    row = lambda b, i: (b, lat(i), 0)
    vec = lambda b, i: (b, 0, 0)
    head_q = lambda b, i: (b, 0, lat(i), 0)
    head_c = lambda b, i: (b, 0, cxt(i), 0)
    head_k = lambda b, i: (b, 0, i, 0)
    tab = lambda b, i: (lat(i), 0)
    rw = MLA_HEADS * QK_ROPE
    return pl.pallas_call(
        functools.partial(_od_lat_kernel, n_ctx_tiles=n_off),
        grid=(nb, n_off + t // tm),
        in_specs=[pl.BlockSpec((1, tm, d), row), pl.BlockSpec((1, 1, d), vec), pl.BlockSpec((1, 1, d), vec)]
                 + [_full(p.shape) for p in params]
                 + [pl.BlockSpec((tm, rw), tab), pl.BlockSpec((tm, rw), tab),
                    pl.BlockSpec((1, MLA_HEADS, tm, QK_DIM), head_c),
                    pl.BlockSpec((1, MLA_HEADS, tm, V_COLS), head_c)],
        out_specs=[pl.BlockSpec((1, MLA_HEADS, tm, QK_DIM), head_q),
                   pl.BlockSpec((1, MLA_HEADS, tm, QK_DIM), head_k),
                   pl.BlockSpec((1, MLA_HEADS, tm, V_COLS), head_k),
                   pl.BlockSpec((1, tm, MLA_WIDTH), row)],
        out_shape=[jax.ShapeDtypeStruct((nb, MLA_HEADS, t, QK_DIM), BF16),
                   jax.ShapeDtypeStruct((nb, MLA_HEADS, tc + t, QK_DIM), BF16),
                   jax.ShapeDtypeStruct((nb, MLA_HEADS, tc + t, V_COLS), BF16),
                   jax.ShapeDtypeStruct((nb, t, MLA_WIDTH), BF16)],
        compiler_params=_cparams("parallel", "arbitrary"),
        name="od_in_lat",
    )(x, scale, shift, *params, cos, sin, k_ctx, v_ctx)


ATTN_SUB = 512


def _attn_kernel(q_ref, k_ref, v_ref, o_ref):
    k = k_ref[0, 0]
    v = v_ref[0, 0]
    sub = min(ATTN_SUB, q_ref.shape[2])
    n_sub = q_ref.shape[2] // sub

    def scores(i):
        return lax.dot_general(q_ref[0, 0, i * sub:(i + 1) * sub, :], k, (((1,), (1,)), ((), ())),
                               preferred_element_type=F32)

    s = scores(0)
    for i in range(n_sub):
        s_next = scores(i + 1) if i + 1 < n_sub else None
        p = jnp.exp2((s - jnp.max(s, axis=-1, keepdims=True)).astype(BF16))
        ov = jnp.dot(p, v, preferred_element_type=F32)
        o_ref[0, i * sub:(i + 1) * sub, :] = (ov[:, :V_HEAD] / ov[:, V_HEAD:]).astype(BF16)
        s = s_next


def _attn_call(q, k, v, tq):
    nb, nh, t, _ = q.shape
    tk = k.shape[2]
    whole = lambda b, h, i: (b, h, 0, 0)
    return pl.pallas_call(
        _attn_kernel,
        grid=(nb, nh, t // tq),
        in_specs=[pl.BlockSpec((1, 1, tq, QK_DIM), lambda b, h, i: (b, h, i, 0)),
                  pl.BlockSpec((1, 1, tk, QK_DIM), whole),
                  pl.BlockSpec((1, 1, tk, V_COLS), whole)],
        out_specs=pl.BlockSpec((1, tq, V_HEAD), lambda b, h, i: (b, i, h)),
        out_shape=jax.ShapeDtypeStruct((nb, t, nh * V_HEAD), BF16),
        compiler_params=_cparams("parallel", "parallel", "parallel"),
        name="mla_attn",
    )(q, k, v)


def _od_out_kernel(o_ref, z_ref, wo_ref, x_ref, g_ref, out_ref):
    a = (o_ref[0].astype(F32) * _silu(z_ref[0].astype(F32))).astype(BF16)
    out_ref[0] = x_ref[0] + g_ref[0] * jnp.dot(a, wo_ref[...], preferred_element_type=F32)


def _od_out_call(o, z, wo_bf16, x, gate, tm):
    nb, t, d = x.shape
    row = lambda b, i: (b, i, 0)
    return pl.pallas_call(
        _od_out_kernel,
        grid=(nb, t // tm),
        in_specs=[pl.BlockSpec((1, tm, MLA_WIDTH), row), pl.BlockSpec((1, tm, MLA_WIDTH), row),
                  _full(wo_bf16.shape), pl.BlockSpec((1, tm, d), row),
                  pl.BlockSpec((1, 1, d), lambda b, i: (b, 0, 0))],
        out_specs=pl.BlockSpec((1, tm, d), row),
        out_shape=jax.ShapeDtypeStruct((nb, t, d), F32),
        compiler_params=_cparams("parallel", "parallel"),
        name="od_out",
    )(o, z, wo_bf16, x, gate)


def _rope_tables(n):
    rows = n // GRID_W
    row = jnp.repeat(jnp.arange(rows, dtype=F32), GRID_W)
    col = jnp.tile(jnp.arange(GRID_W, dtype=F32), rows)
    inv_freq = ROPE_THETA ** (-jnp.arange(ROPE_PAIRS, dtype=F32) / ROPE_PAIRS)
    ar = row[:, None] * inv_freq
    ac = col[:, None] * inv_freq
    cos = jnp.concatenate([jnp.cos(ar), jnp.cos(ar), jnp.cos(ac), jnp.cos(ac)], axis=1)
    sin = jnp.concatenate([-jnp.sin(ar), jnp.sin(ar), -jnp.sin(ac), jnp.sin(ac)], axis=1)
    return cos, sin


def _row_tile(t, want):
    tm = min(t, want)
    assert t % tm == 0 and tm % HALO == 0
    return tm


def kernel(x, c, ctx, c_ctx, ada_w, ada_b, norm_w, ev_w_in, ev_conv_w, ev_mu, ev_k_k, ev_k_a, ev_w0, ev_w2, ev_a0, ev_a2, ev_r_k, ev_lnx_w, ev_lnx_b, ev_w_out, od_w_in, od_q_a_norm, od_kv_a_norm, od_w_qb, od_w_kvb, od_gq_nope, od_gq_rope, od_gk_nope, od_gk_rope, od_w_o):
    nb, t, d = x.shape
    tc = ctx.shape[1]
    assert t % CHUNK == 0 and tc % CHUNK == 0 and t % GRID_W == 0

    rows = -(-(nb + 1) // HALO) * HALO
    cc = jnp.zeros((rows, d), F32).at[:nb].set(c).at[nb].set(c_ctx)
    mod = _ada_call(cc, ada_w, ada_b)

    def mods(layer):
        m = mod[layer]
        lat = [m[:nb, i * d:(i + 1) * d].reshape(nb, 1, d) for i in range(3)]
        cx = [jnp.broadcast_to(m[nb, i * d:(i + 1) * d].reshape(1, 1, d), (nb, 1, d)) for i in range(3)]
        return lat, cx

    e_np = np.arange(PAIR)[:, None] // RWKV_HEAD == np.arange(PAIR)[None, :] // RWKV_HEAD
    e_bf16 = jnp.asarray(e_np, BF16)
    ones_bf16 = jnp.ones((LANES, LANES), BF16)

    (sh_l, sc_l, g_l), (sh_c, sc_c, g_c) = mods(0)
    nw0 = norm_w[0].reshape(1, d)
    w_in0 = ev_w_in[0].astype(BF16)
    tm_l = _row_tile(t, 256)
    tm_c = _row_tile(tc, 256)
    pa_c, pr_c, zb_c = _ev_in_call(ctx, sc_c, sh_c, nw0, w_in0, tm_c)
    pa_l, pr_l, zb_l = _ev_in_call(x, sc_l, sh_l, nw0, w_in0, tm_l)

    W = RWKV_WIDTH
    vec = lambda a: a.reshape(1, -1)
    lora = jnp.zeros((2, 2 * LORA, 2 * W), F32)
    lora = lora.at[:, :LORA, :W].set(ev_w2[0]).at[:, LORA:, W:].set(ev_a2[0])
    lora_hi = lora.astype(BF16)
    lora_lo = (lora - lora_hi.astype(F32)).astype(BF16)
    wa0 = jnp.concatenate([ev_w0[0], ev_a0[0]], axis=-1).reshape(2, 1, 2 * W)
    params = [vec(1.0 - ev_mu[0]), vec(0.5 * ev_mu[0]), vec(ev_k_k[0]), vec(ev_k_a[0]), vec(1.0 - ev_k_a[0]),
              vec(ev_r_k[0]), wa0, lora_hi, lora_lo, e_bf16]
    yf_c, yb_c, bf_c, bb_c, s_c = _scan_call(pr_c, params, None)
    yf_l, yb_l, bf_l, bb_l, _ = _scan_call(pr_l, params, s_c)

    wo0 = ev_w_out[0].astype(BF16)
    lw, lb = vec(ev_lnx_w[0]), vec(ev_lnx_b[0])
    x1 = _ev_out_call(pa_l, zb_l, yf_l, yb_l, bf_l, bb_l, ev_conv_w[0], lw, lb, e_bf16, wo0, x, g_l,
                      _row_tile(t, 512))
    ctx1 = _ev_out_call(pa_c, zb_c, yf_c, yb_c, bf_c, bb_c, ev_conv_w[0], lw, lb, e_bf16, wo0, ctx, g_c, tm_c)

    (sh_l, sc_l, g_l), (sh_c, sc_c, _) = mods(1)
    nw1 = norm_w[1].reshape(1, d)
    swap = np.arange(QK_ROPE) ^ ROPE_PAIRS
    hq = np.arange(MLA_HEADS)[:, None] * QK_DIM
    idx_q = np.concatenate([(hq + np.arange(QK_NOPE)).ravel(),
                            (hq + QK_NOPE + np.arange(QK_ROPE)).ravel(),
                            (hq + QK_NOPE + swap).ravel()])
    hk = np.arange(MLA_HEADS)[:, None] * (QK_NOPE + V_HEAD)
    idx_kv = np.concatenate([(hk + np.arange(QK_NOPE)).ravel(), (hk + QK_NOPE + np.arange(V_HEAD)).ravel()])
    r_off = Q_LORA + KV_LORA
    idx_in = np.concatenate([np.arange(r_off + QK_ROPE), r_off + swap,
                             np.arange(r_off + QK_ROPE, od_w_in.shape[2])])
    w_in1 = od_w_in[0][:, idx_in].astype(BF16)
    wqb = od_w_qb[0][:, idx_q].astype(BF16)
    wkvb = od_w_kvb[0][:, idx_kv].astype(BF16)
    gqr = jnp.tile(od_gq_rope[0], MLA_HEADS).reshape(1, -1)
    gqrs = jnp.tile(od_gq_rope[0][swap], MLA_HEADS).reshape(1, -1)
    gkr, gkrs = vec(od_gk_rope[0]), vec(od_gk_rope[0][swap])
    gqn, gkn = vec(od_gq_nope[0]), vec(od_gk_nope[0])
    qan, kvn = vec(od_q_a_norm[0]), vec(od_kv_a_norm[0])
    cos, sin = _rope_tables(t)
    cos = jnp.tile(cos, (1, MLA_HEADS))
    sin = jnp.tile(sin, (1, MLA_HEADS))

    k_c, v_c = _od_ctx_call(ctx1, sc_c, sh_c, nw1, w_in1[:, Q_LORA:Q_LORA + KV_LORA + 2 * QK_ROPE],
                            kvn, wkvb, gkn, gkr, gkrs, ones_bf16, _row_tile(tc, 256))
    params = [nw1, w_in1, qan, kvn, wqb, wkvb, gqn, gqr, gqrs, gkn, gkr, gkrs, ones_bf16, e_bf16]
    q, k_all, v_all, z = _od_lat_call(x1, sc_l, sh_l, params, cos, sin, k_c, v_c, _row_tile(tc, 256))
    o = _attn_call(q, k_all, v_all, _row_tile(t, 2 * ATTN_SUB))
    return _od_out_call(o, z, od_w_o[0].astype(BF16), x1, g_l, _row_tile(t, 512))
```

```python
import functools

import numpy as np
import jax
import jax.numpy as jnp
from jax import lax
from jax.experimental import pallas as pl
from jax.experimental.pallas import tpu as pltpu

F32 = jnp.float32
BF16 = jnp.bfloat16

NORM_EPS = 1e-6
GRID_W = 64

CONV_WIDTH = 512
RWKV_HEAD = 64
RWKV_WIDTH = 512
LORA = 64
RWKV_GN_EPS = 64e-5
DECAY_SCALE = -float(np.exp(-0.5))
A_COLS = 4 * CONV_WIDTH
SHIFT_COLS = 3 * RWKV_WIDTH + 2 * LORA
CHUNK = 64
PAIR = 2 * RWKV_HEAD
N_PAIRS = RWKV_WIDTH // PAIR

MLA_HEADS = 8
QK_NOPE = 128
QK_ROPE = 64
QK_DIM = QK_NOPE + QK_ROPE
V_HEAD = 128
Q_LORA = 384
KV_LORA = 256
MLA_WIDTH = MLA_HEADS * V_HEAD
SM_SCALE = QK_DIM ** -0.5
LOG2E = 1.4426950408889634
Q_SCALE = SM_SCALE * LOG2E
V_COLS = 2 * V_HEAD
ROPE_THETA = 10000.0
ROPE_PAIRS = QK_ROPE // 4
LANES = 128

VMEM_LIMIT_BYTES = 56 * 1024 * 1024
HALO = 8
HALO_BF16 = 16


def _cparams(*sem):
    return pltpu.CompilerParams(dimension_semantics=sem, vmem_limit_bytes=VMEM_LIMIT_BYTES)


def _silu(x):
    return x / (1.0 + jnp.exp(-x))


def _dot(a, b):
    return jnp.dot(a.astype(BF16), b.astype(BF16), preferred_element_type=F32)


def _dot_nt(a, b):
    return lax.dot_general(a.astype(BF16), b.astype(BF16), (((1,), (1,)), ((), ())),
                           preferred_element_type=F32)


def _bf16_split(x):
    hi = x.astype(BF16)
    return hi, (x - hi.astype(F32)).astype(BF16)


def _mm(a, b):
    return jnp.dot(a, b, preferred_element_type=F32)


def _dot3(a, b_hi, b_lo):
    a_hi, a_lo = _bf16_split(a)
    return _mm(a_hi, b_hi) + (_mm(a_lo, b_hi) + _mm(a_hi, b_lo))


def _exact_lhs_dot(m_bf16, x):
    hi, lo = _bf16_split(x)
    return _mm(m_bf16, hi) + _mm(m_bf16, lo)


def _segsum(x, e2, passes=2):
    out = []
    for i in range(x.shape[1] // PAIR):
        hi, lo = _bf16_split(x[:, i * PAIR:(i + 1) * PAIR])
        out.append(_mm(hi, e2) + _mm(lo, e2) if passes == 2 else _mm(hi, e2))
    return jnp.concatenate(out, axis=1)


def _full(shape):
    return pl.BlockSpec(shape, lambda *_: (0,) * len(shape))


def _ada_kernel(c_ref, w_ref, b_ref, o_ref):
    o_ref[0] = _dot3(_silu(c_ref[...]), *_bf16_split(w_ref[0])) + b_ref[0]


def _ada_call(cc, ada_w, ada_b):
    depth, d, d3 = ada_w.shape
    rows = cc.shape[0]
    tn = 1024
    return pl.pallas_call(
        _ada_kernel,
        grid=(depth, d3 // tn),
        in_specs=[pl.BlockSpec((rows, d), lambda l, j: (0, 0)),
                  pl.BlockSpec((1, d, tn), lambda l, j: (l, 0, j)),
                  pl.BlockSpec((1, 1, tn), lambda l, j: (l, 0, j))],
        out_specs=pl.BlockSpec((1, rows, tn), lambda l, j: (l, 0, j)),
        out_shape=jax.ShapeDtypeStruct((depth, rows, d3), F32),
        compiler_params=_cparams("arbitrary", "arbitrary"),
        name="ada_mod",
    )(cc, ada_w, ada_b.reshape(depth, 1, d3))


def _modulated_norm(x, nw, scale, shift):
    rs = lax.rsqrt(jnp.mean(x * x, axis=-1, keepdims=True) + NORM_EPS)
    return (x * rs * nw) * (1.0 + scale) + shift


def _ev_in_kernel(x_ref, sc_ref, sh_ref, nw_ref, w_ref, pa_ref, pr_ref, zb_ref):
    h = _modulated_norm(x_ref[0], nw_ref[...], sc_ref[0], sh_ref[0]).astype(BF16)
    pa_ref[0] = jnp.dot(h, w_ref[:, :A_COLS], preferred_element_type=F32).astype(BF16)
    pr_ref[0] = jnp.dot(h, w_ref[:, A_COLS:A_COLS + SHIFT_COLS], preferred_element_type=F32)
    zb_ref[0] = jnp.dot(h, w_ref[:, A_COLS + SHIFT_COLS:], preferred_element_type=F32).astype(BF16)


def _ev_in_call(x, scale, shift, nw, w_bf16, tm):
    nb, t, d = x.shape
    ncol = w_bf16.shape[1]
    row = lambda b, i: (b, i, 0)
    vec = lambda b, i: (b, 0, 0)
    return pl.pallas_call(
        _ev_in_kernel,
        grid=(nb, t // tm),
        in_specs=[pl.BlockSpec((1, tm, d), row),
                  pl.BlockSpec((1, 1, d), vec),
                  pl.BlockSpec((1, 1, d), vec),
                  _full((1, d)),
                  _full((d, ncol))],
        out_specs=[pl.BlockSpec((1, tm, A_COLS), row),
                   pl.BlockSpec((1, tm, SHIFT_COLS), row),
                   pl.BlockSpec((1, tm, RWKV_WIDTH), row)],
        out_shape=[jax.ShapeDtypeStruct((nb, t, A_COLS), BF16),
                   jax.ShapeDtypeStruct((nb, t, SHIFT_COLS), F32),
                   jax.ShapeDtypeStruct((nb, t, RWKV_WIDTH), BF16)],
        compiler_params=_cparams("parallel", "parallel"),
        name="ev_in",
    )(x, scale, shift, nw, w_bf16)


def _neighbours(cur, prev_halo, next_halo, is_first, is_last):
    n = cur.shape[0]
    rowi = lax.broadcasted_iota(jnp.int32, cur.shape, 0)
    prev_row = jnp.where(is_first, 0.0, prev_halo[prev_halo.shape[0] - 1:, :])
    next_row = jnp.where(is_last, 0.0, next_halo[0:1, :])
    prev = jnp.where(rowi == 0, prev_row, pltpu.roll(cur, 1, 0))
    nxt = jnp.where(rowi == n - 1, next_row, pltpu.roll(cur, n - 1, 0))
    return prev, nxt


OPERANDS = ("rh", "ah", "bc", "kc", "bt", "kt", "v", "bon")


def _scan_prep_pieces(reverse, cur_ref, prv_ref, nxt_ref, is_first, is_last, prm, w_dir, ops_ref, gc_ref, bi, d):
    C = CHUNK
    W = RWKV_WIDTH
    mu_keep, mu_half, k_k, k_a, ka_keep, r_k, e2 = prm
    wa0, lora_hi, lora_lo = w_dir
    t = {}

    def shift():
        pr = cur_ref[bi]
        prev, nxt = _neighbours(pr, prv_ref[bi], nxt_ref[bi], is_first, is_last)
        x = mu_keep * pr + mu_half * (prev + nxt)
        t.update(r=x[:, 0:W], k=x[:, W:2 * W], v=x[:, 2 * W:3 * W], wa=x[:, 3 * W:3 * W + 2 * LORA])

    def rates():
        lane = lax.broadcasted_iota(jnp.int32, (C, PAIR), 1)
        tw = jnp.where(lane < LORA, jnp.tanh(t["wa"]), t["wa"])
        lo = _dot3(tw, lora_hi, lora_lo)
        sig = 1.0 / (1.0 + jnp.exp(-(wa0 + lo)))
        t.update(lw=DECAY_SCALE * sig[:, :W], iclr=sig[:, W:])

    def keys():
        k, iclr = t["k"], t["iclr"]
        kkv = k * k_k
        kkn = kkv * lax.rsqrt(jnp.maximum(_segsum(kkv * kkv, e2, passes=1), 1e-24))
        k_dir = k * (ka_keep + iclr * k_a)
        t.update(kkn=kkn, k_dir=k_dir, bvec=kkn * iclr)
        ops_ref[bi, d, OPERANDS.index("bon")] = _segsum(t["r"] * k_dir * r_k, e2) * t["v"]

    def decays():
        lw = t["lw"]
        ti = lax.broadcasted_iota(jnp.int32, (C, C), 0)
        si = lax.broadcasted_iota(jnp.int32, (C, C), 1)
        tri = ((si >= ti) if reverse else (si <= ti)).astype(BF16)
        g_in = _exact_lhs_dot(tri, lw)
        g_tot = jnp.sum(lw, axis=0, keepdims=True)
        t.update(gam=jnp.exp(g_in), gam_ex=jnp.exp(g_in - lw), gam_inv=jnp.exp(-g_in), gam_c=jnp.exp(g_tot))

    def store():
        gam_c = t["gam_c"]
        bc = t["bvec"] * t["gam_inv"]
        kc = t["k_dir"] * t["gam_inv"]
        out = dict(rh=t["r"] * t["gam"], ah=-t["kkn"] * t["gam_ex"], bc=bc, kc=kc,
                   bt=bc * gam_c, kt=kc * gam_c, v=t["v"])
        for name, val in out.items():
            ops_ref[bi, d, OPERANDS.index(name)] = val
        gc_ref[bi, d] = gam_c

    return [shift, rates, keys, decays, store]


def _scan_kernel(n_chunks, has_s0, n_bb, *refs):
    refs = list(refs)
    first_refs = (refs[0:3], refs[3:6])
    next_refs = (refs[6:9], refs[9:12])
    (muk_ref, muh_ref, kk_ref, ka_ref, kak_ref, rk_ref, wa0_ref, lh_ref, ll_ref, e_ref) = refs[12:22]
    pos = 22
    s0_ref = None
    if has_s0:
        s0_ref = refs[pos]
        pos += 1
    y_refs = refs[pos:pos + 2]
    bon_refs = refs[pos + 2:pos + 4]
    sfin_ref, st_ref, ops_a, gc_a, ops_b, gc_b = refs[pos + 4:pos + 10]

    C = CHUNK
    c = pl.program_id(1)
    prm = (muk_ref[...], muh_ref[...], kk_ref[...], ka_ref[...], kak_ref[...], rk_ref[...], e_ref[...])

    def pieces(d, src, is_first, is_last, ops_dst, gc_dst):
        cur, prv, nxt = src
        per_b = [_scan_prep_pieces(d == 1, cur, prv, nxt, is_first, is_last, prm,
                                   (wa0_ref[d], lh_ref[d], ll_ref[d]), ops_dst, gc_dst, bi, d) for bi in range(n_bb)]
        return [fn for group in zip(*per_b) for fn in group]

    @pl.when(c == 0)
    def _():
        if has_s0:
            st_ref[...] = s0_ref[...]
        else:
            st_ref[...] = jnp.zeros_like(st_ref)
        for piece in pieces(0, first_refs[0], True, n_chunks == 1, ops_a, gc_a) + \
                pieces(1, first_refs[1], n_chunks == 1, True, ops_a, gc_a):
            piece()

    nxt_f = jnp.minimum(c + 1, n_chunks - 1)
    nxt_b = jnp.maximum(n_chunks - 2 - c, 0)

    def chain(ops_ref, gc_ref, overlap):
        n2 = 2 * C
        ri = lax.broadcasted_iota(jnp.int32, (n2, n2), 0)
        ci = lax.broadcasted_iota(jnp.int32, (n2, n2), 1)
        same_head = (ri >= C) == (ci >= C)
        rt = ri & (C - 1)
        ct = ci & (C - 1)
        strict = (same_head & (ct < rt), same_head & (ct > rt))
        incl = (same_head & (ct <= rt), same_head & (ct >= rt))
        diag = ri == ci
        eye = diag.astype(F32)
        blocks = [same_head & ((rt >> s) == (ct >> s)) for s in range(1, C.bit_length())]
        first_head = lax.broadcasted_iota(jnp.int32, (C, PAIR), 1) < RWKV_HEAD

        def stk(z):
            return jnp.concatenate([jnp.where(first_head, z, 0.0), jnp.where(first_head, 0.0, z)], axis=0)

        def dup(z):
            return jnp.concatenate([z, z], axis=0)

        chains = [(bi, d, slice(p * PAIR, (p + 1) * PAIR), p)
                  for bi in range(n_bb) for d in range(2) for p in range(N_PAIRS)]
        n_ch = len(chains)

        def operand(name, bi, d, sl):
            return ops_ref[bi, d, OPERANDS.index(name), :, sl]

        ah_s = [stk(operand("ah", bi, d, sl)) for bi, d, sl, _ in chains]
        rh_s = [stk(operand("rh", bi, d, sl)) for bi, d, sl, _ in chains]
        v_s = [stk(operand("v", bi, d, sl)) for bi, d, sl, _ in chains]
        pm = [_dot_nt(jnp.concatenate([ah_s[i], rh_s[i]], axis=0),
                      jnp.concatenate([dup(operand("bc", bi, d, sl)), dup(operand("kc", bi, d, sl))], axis=0))
              for i, (bi, d, sl, _) in enumerate(chains)]
        overlap()
        a_ab = [jnp.where(strict[d], pm[i][:n2, :n2], 0.0) for i, (_, d, _, _) in enumerate(chains)]
        a_ak = [jnp.where(strict[d], pm[i][:n2, n2:], 0.0) for i, (_, d, _, _) in enumerate(chains)]
        a_r = [jnp.concatenate([jnp.where(incl[d], pm[i][n2:, :n2], 0.0),
                                jnp.where(incl[d], pm[i][n2:, n2:], 0.0)], axis=1)
               for i, (_, d, _, _) in enumerate(chains)]
        akv = [_dot(a_ak[i], v_s[i]) for i in range(n_ch)]
        tinv = [jnp.where(blocks[0], a, 0.0) + eye for a in a_ab]
        overlap()
        for lvl in range(1, len(blocks)):
            off = blocks[lvl] & ~blocks[lvl - 1]
            s = 1 << lvl
            if s < HALO:
                t1 = [_dot(jnp.where(off, a_ab[i], 0.0), tinv[i]) for i in range(n_ch)]
                overlap()
                t2 = [_dot(tinv[i], t1[i]) for i in range(n_ch)]
                tinv = [tinv[i] + t2[i] for i in range(n_ch)]
            else:
                def band(x, d):
                    return jnp.concatenate([x[j * s:(j + 1) * s] for j in range(1 - d, n2 // s, 2)], axis=0)

                def unband(x, d, rest):
                    return jnp.concatenate([x[(j // 2) * s:(j // 2 + 1) * s] if j % 2 == 1 - d
                                            else rest[j * s:(j + 1) * s] for j in range(n2 // s)], axis=0)

                t1 = [_dot(band(jnp.where(off, a_ab[i], 0.0), d), tinv[i]) for i, (_, d, _, _) in enumerate(chains)]
                overlap()
                zero = jnp.zeros((n2, n2), F32)
                t2 = [_dot(band(tinv[i], d), unband(t1[i], d, zero)) for i, (_, d, _, _) in enumerate(chains)]
                tinv = [unband(band(tinv[i], d) + t2[i], d, tinv[i]) for i, (_, d, _, _) in enumerate(chains)]
            overlap()
        xx = [_dot(tinv[i], jnp.concatenate([ah_s[i], akv[i]], axis=1)) for i in range(n_ch)]
        zeros = jnp.zeros((n2, PAIR), F32)
        rr = []
        for i, (bi, d, sl, _) in enumerate(chains):
            rhs = jnp.concatenate([xx[i], jnp.concatenate([zeros, v_s[i]], axis=1)], axis=0)
            lhs = jnp.concatenate(
                [a_r[i], jnp.concatenate([stk(operand("bt", bi, d, sl)).T, stk(operand("kt", bi, d, sl)).T], axis=1)],
                axis=0)
            rr.append(_dot(lhs, rhs))
        for i, (bi, d, sl, p) in enumerate(chains):
            q_t = rh_s[i] + rr[i][:n2, :PAIR]
            st = st_ref[bi, d, p]
            st_hi, st_lo = _bf16_split(st)
            m_off = rr[i][n2:, :PAIR].astype(BF16)
            qm = _mm(jnp.concatenate([q_t.astype(BF16), m_off], axis=0), st_hi)
            ys = qm[:n2] + rr[i][:n2, PAIR:]
            g_col = jnp.sum(jnp.where(diag, gc_ref[bi, d][:, sl], 0.0), axis=1, keepdims=True)
            st_ref[bi, d, p] = (g_col * st + (qm[n2:] + _mm(m_off, st_lo))) + rr[i][n2:, PAIR:]
            y_refs[d][bi, :, sl] = ys[:C] + ys[C:]
        for bi in range(n_bb):
            for d in range(2):
                bon_refs[d][bi] = ops_ref[bi, d, OPERANDS.index("bon")]

    def step(ops_ref, gc_ref, ops_nxt, gc_nxt):
        prep_f = pieces(0, next_refs[0], nxt_f == 0, nxt_f == n_chunks - 1, ops_nxt, gc_nxt)
        prep_b = pieces(1, next_refs[1], nxt_b == 0, nxt_b == n_chunks - 1, ops_nxt, gc_nxt)
        background = [fn for pair in zip(prep_f, prep_b) for fn in pair]

        per_call = -(-len(background) // 12)

        def overlap():
            for _ in range(per_call):
                if background:
                    background.pop(0)()

        chain(ops_ref, gc_ref, overlap)
        while background:
            overlap()

    @pl.when(c % 2 == 0)
    def _():
        step(ops_a, gc_a, ops_b, gc_b)

    @pl.when(c % 2 == 1)
    def _():
        step(ops_b, gc_b, ops_a, gc_a)

    @pl.when(c == n_chunks - 1)
    def _():
        sfin_ref[...] = st_ref[...]


SCAN_BATCH_BLOCK = 2


def _scan_call(pr, params, s0):
    nb, t, _ = pr.shape
    bb = SCAN_BATCH_BLOCK if nb % SCAN_BATCH_BLOCK == 0 else 1
    C = CHUNK
    n_chunks = t // C
    hb = C // HALO
    n_hblk = t // HALO

    def specs(cs_of):
        return [pl.BlockSpec((bb, C, SHIFT_COLS), lambda b, c: (b, cs_of(c), 0)),
                pl.BlockSpec((bb, HALO, SHIFT_COLS), lambda b, c: (b, jnp.maximum(cs_of(c) * hb - 1, 0), 0)),
                pl.BlockSpec((bb, HALO, SHIFT_COLS), lambda b, c: (b, jnp.minimum((cs_of(c) + 1) * hb, n_hblk - 1), 0))]

    first_f = lambda c: 0
    first_b = lambda c: n_chunks - 1
    next_f = lambda c: jnp.minimum(c + 1, n_chunks - 1)
    next_b = lambda c: jnp.maximum(n_chunks - 2 - c, 0)
    st_spec = pl.BlockSpec((bb, 2, N_PAIRS, PAIR, PAIR), lambda b, c: (b, 0, 0, 0, 0))
    in_specs = specs(first_f) + specs(first_b) + specs(next_f) + specs(next_b) + [_full(p.shape) for p in params]
    args = [pr] * 12 + list(params)
    if s0 is not None:
        in_specs.append(st_spec)
        args.append(s0)
    out_f = pl.BlockSpec((bb, C, RWKV_WIDTH), lambda b, c: (b, c, 0))
    out_b = pl.BlockSpec((bb, C, RWKV_WIDTH), lambda b, c: (b, n_chunks - 1 - c, 0))
    tok = jax.ShapeDtypeStruct((nb, t, RWKV_WIDTH), F32)
    return pl.pallas_call(
        functools.partial(_scan_kernel, n_chunks, s0 is not None, bb),
        grid=(nb // bb, n_chunks),
        in_specs=in_specs,
        out_specs=[out_f, out_b, out_f, out_b, st_spec],
        out_shape=[tok, tok, tok, tok, jax.ShapeDtypeStruct((nb, 2, N_PAIRS, PAIR, PAIR), F32)],
        scratch_shapes=[pltpu.VMEM((bb, 2, N_PAIRS, PAIR, PAIR), F32),
                        pltpu.VMEM((bb, 2, len(OPERANDS), C, RWKV_WIDTH), F32), pltpu.VMEM((bb, 2, 1, RWKV_WIDTH), F32),
                        pltpu.VMEM((bb, 2, len(OPERANDS), C, RWKV_WIDTH), F32), pltpu.VMEM((bb, 2, 1, RWKV_WIDTH), F32)],
        compiler_params=_cparams("parallel", "arbitrary"),
        name="rwkv_scan",
    )(*args)


def _ev_out_kernel(n_tiles, pa_ref, pap_ref, pan_ref, zb_ref, yf_ref, yb_ref, bf_ref, bb_ref,
                   cw_ref, lw_ref, lb_ref, e_ref, wo_ref, x_ref, g_ref, o_ref):
    i = pl.program_id(1)
    W = CONV_WIDTH
    pa = pa_ref[0].astype(F32)
    u, gate_b, gate_c, z = pa[:, 0:W], pa[:, W:2 * W], pa[:, 2 * W:3 * W], pa[:, 3 * W:4 * W]
    cu = gate_c * u
    pa_p = pap_ref[0].astype(F32)
    pa_n = pan_ref[0].astype(F32)
    cu_p = pa_p[:, 2 * W:3 * W] * pa_p[:, 0:W]
    cu_n = pa_n[:, 2 * W:3 * W] * pa_n[:, 0:W]
    prev, nxt = _neighbours(cu, cu_p, cu_n, i == 0, i == n_tiles - 1)
    cw = cw_ref[...]
    conv = cw[0:1] * prev + cw[1:2] * cu + cw[2:3] * nxt
    a_out = gate_b * conv * _silu(z)

    e = e_ref[...]
    ysum = yf_ref[0] + yb_ref[0]
    inv_n = 1.0 / RWKV_HEAD
    mean = _segsum(ysum, e) * inv_n
    dlt = ysum - mean
    var = _segsum(dlt * dlt, e) * inv_n
    rw = dlt * lax.rsqrt(var + RWKV_GN_EPS) * lw_ref[...] + lb_ref[...] + (bf_ref[0] + bb_ref[0])
    b_out = rw * _silu(zb_ref[0].astype(F32))

    y = (jnp.dot(a_out.astype(BF16), wo_ref[:W, :], preferred_element_type=F32)
         + jnp.dot(b_out.astype(BF16), wo_ref[W:, :], preferred_element_type=F32))
    o_ref[0] = x_ref[0] + g_ref[0] * y


def _ev_out_call(pa, zb, yf, yb, bf, bb, conv_w, lnx_w, lnx_b, e_bf16, wo_bf16, x, gate, tm):
    nb, t, d = x.shape
    n_tiles = t // tm
    hb = tm // HALO_BF16
    n_hblk = t // HALO_BF16
    row = lambda b, i: (b, i, 0)
    prv = lambda b, i: (b, jnp.maximum(i * hb - 1, 0), 0)
    nxt = lambda b, i: (b, jnp.minimum((i + 1) * hb, n_hblk - 1), 0)
    vec = lambda b, i: (b, 0, 0)
    wspec = pl.BlockSpec((1, tm, RWKV_WIDTH), row)
    return pl.pallas_call(
        functools.partial(_ev_out_kernel, n_tiles),
        grid=(nb, n_tiles),
        in_specs=[pl.BlockSpec((1, tm, A_COLS), row),
                  pl.BlockSpec((1, HALO_BF16, A_COLS), prv),
                  pl.BlockSpec((1, HALO_BF16, A_COLS), nxt),
                  wspec, wspec, wspec, wspec, wspec,
                  _full(conv_w.shape), _full(lnx_w.shape), _full(lnx_b.shape),
                  _full(e_bf16.shape), _full(wo_bf16.shape),
                  pl.BlockSpec((1, tm, d), row),
                  pl.BlockSpec((1, 1, d), vec)],
        out_specs=pl.BlockSpec((1, tm, d), row),
        out_shape=jax.ShapeDtypeStruct((nb, t, d), F32),
        compiler_params=_cparams("parallel", "parallel"),
        name="ev_out",
    )(pa, pa, pa, zb, yf, yb, bf, bb, conv_w, lnx_w, lnx_b, e_bf16, wo_bf16, x, gate)


def _head_rms(t, g):
    return t * lax.rsqrt(jnp.mean(t * t, axis=-1, keepdims=True) + NORM_EPS) * g


def _lane_rms_scale(t, ones_bf16, n):
    ss = jnp.dot((t * t).astype(BF16), ones_bf16, preferred_element_type=F32)
    return lax.rsqrt(ss * (1.0 / n) + NORM_EPS)


def _keys_values(p_kv, kvn, wkvb, gkn, gkr, gkr_sw, cos, sin, ones, k_ref, v_ref):
    kv_a = p_kv[:, :KV_LORA]
    kr = p_kv[:, KV_LORA:KV_LORA + QK_ROPE]
    kr_sw = p_kv[:, KV_LORA + QK_ROPE:KV_LORA + 2 * QK_ROPE]
    rs = lax.rsqrt(jnp.mean(kr * kr, axis=-1, keepdims=True) + NORM_EPS)
    if cos is None:
        k_rope = kr * rs * gkr
    else:
        k_rope = ((kr * gkr) * cos + (kr_sw * gkr_sw) * sin) * rs
    k_rope = k_rope.astype(BF16)
    kv = jnp.dot(_head_rms(kv_a, kvn).astype(BF16), wkvb, preferred_element_type=F32)
    for h in range(MLA_HEADS):
        kh = kv[:, h * QK_NOPE:(h + 1) * QK_NOPE]
        k_ref[0, h, :, 0:QK_NOPE] = (kh * _lane_rms_scale(kh, ones, QK_NOPE) * gkn).astype(BF16)
        k_ref[0, h, :, QK_NOPE:QK_DIM] = k_rope
        v_ref[0, h, :, 0:V_HEAD] = kv[:, MLA_WIDTH + h * V_HEAD:MLA_WIDTH + (h + 1) * V_HEAD].astype(BF16)
        v_ref[0, h, :, V_HEAD:V_COLS] = jnp.ones((kv.shape[0], V_HEAD), BF16)


def _od_lat_kernel(x_ref, sc_ref, sh_ref, nw_ref, win_ref, qan_ref, kvn_ref, wqb_ref, wkvb_ref,
                   gqn_ref, gqr_ref, gqrs_ref, gkn_ref, gkr_ref, gkrs_ref, ones_ref, e2_ref, cos_ref, sin_ref,
                   kc_ref, vc_ref, q_ref, k_ref, v_ref, z_ref, *, n_ctx_tiles):
    i = pl.program_id(1)

    @pl.when(i < n_ctx_tiles)
    def _():
        k_ref[...] = kc_ref[...]
        v_ref[...] = vc_ref[...]

    @pl.when(i >= n_ctx_tiles)
    def _():
        _od_lat_tile(x_ref, sc_ref, sh_ref, nw_ref, win_ref, qan_ref, kvn_ref, wqb_ref, wkvb_ref,
                     gqn_ref, gqr_ref, gqrs_ref, gkn_ref, gkr_ref, gkrs_ref, ones_ref, e2_ref, cos_ref, sin_ref,
                     q_ref, k_ref, v_ref, z_ref)


def _od_lat_tile(x_ref, sc_ref, sh_ref, nw_ref, win_ref, qan_ref, kvn_ref, wqb_ref, wkvb_ref,
                 gqn_ref, gqr_ref, gqrs_ref, gkn_ref, gkr_ref, gkrs_ref, ones_ref, e2_ref, cos_ref, sin_ref,
                 q_ref, k_ref, v_ref, z_ref):
    h = _modulated_norm(x_ref[0], nw_ref[...], sc_ref[0], sh_ref[0]).astype(BF16)
    p = jnp.dot(h, win_ref[...], preferred_element_type=F32)
    kv_off = Q_LORA
    z_off = Q_LORA + KV_LORA + 2 * QK_ROPE
    z_ref[0] = p[:, z_off:].astype(BF16)
    cos = cos_ref[...]
    sin = sin_ref[...]
    _keys_values(p[:, kv_off:z_off], kvn_ref[...], wkvb_ref[...], gkn_ref[...], gkr_ref[...], gkrs_ref[...],
                 cos[:, :QK_ROPE], sin[:, :QK_ROPE], ones_ref[...], k_ref, v_ref)

    q = jnp.dot(_head_rms(p[:, :Q_LORA], qan_ref[...]).astype(BF16), wqb_ref[...],
                preferred_element_type=F32)
    rw = MLA_HEADS * QK_ROPE
    qr = q[:, MLA_WIDTH:MLA_WIDTH + rw]
    qr_sw = q[:, MLA_WIDTH + rw:MLA_WIDTH + 2 * rw]
    rot = (qr * gqr_ref[...]) * cos + (qr_sw * gqrs_ref[...]) * sin
    ones = ones_ref[...]
    e2 = e2_ref[...]
    gqn = gqn_ref[...] * Q_SCALE
    rot = jnp.concatenate(
        [rot[:, j * LANES:(j + 1) * LANES] * (_lane_rms_scale(qr[:, j * LANES:(j + 1) * LANES], e2, QK_ROPE) * Q_SCALE)
         for j in range(rw // LANES)], axis=1).astype(BF16)
    for hd in range(MLA_HEADS):
        qh = q[:, hd * QK_NOPE:(hd + 1) * QK_NOPE]
        q_ref[0, hd, :, 0:QK_NOPE] = (qh * _lane_rms_scale(qh, ones, QK_NOPE) * gqn).astype(BF16)
        q_ref[0, hd, :, QK_NOPE:QK_DIM] = rot[:, hd * QK_ROPE:(hd + 1) * QK_ROPE]


def _od_ctx_kernel(x_ref, sc_ref, sh_ref, nw_ref, win_ref, kvn_ref, wkvb_ref,
                   gkn_ref, gkr_ref, gkrs_ref, ones_ref, k_ref, v_ref):
    h = _modulated_norm(x_ref[0], nw_ref[...], sc_ref[0], sh_ref[0]).astype(BF16)
    p = jnp.dot(h, win_ref[...], preferred_element_type=F32)
    _keys_values(p, kvn_ref[...], wkvb_ref[...], gkn_ref[...], gkr_ref[...], gkrs_ref[...],
                 None, None, ones_ref[...], k_ref, v_ref)


def _od_ctx_call(ctx, scale, shift, nw, win_kv, kvn, wkvb, gkn, gkr, gkrs, ones, tm):
    nb, tc, d = ctx.shape
    row = lambda b, i: (b, i, 0)
    vec = lambda b, i: (b, 0, 0)
    head = lambda b, i: (b, 0, i, 0)
    params = [nw, win_kv, kvn, wkvb, gkn, gkr, gkrs, ones]
    return pl.pallas_call(
        _od_ctx_kernel,
        grid=(nb, tc // tm),
        in_specs=[pl.BlockSpec((1, tm, d), row), pl.BlockSpec((1, 1, d), vec), pl.BlockSpec((1, 1, d), vec)]
                 + [_full(p.shape) for p in params],
        out_specs=[pl.BlockSpec((1, MLA_HEADS, tm, QK_DIM), head),
                   pl.BlockSpec((1, MLA_HEADS, tm, V_COLS), head)],
        out_shape=[jax.ShapeDtypeStruct((nb, MLA_HEADS, tc, QK_DIM), BF16),
                   jax.ShapeDtypeStruct((nb, MLA_HEADS, tc, V_COLS), BF16)],
        compiler_params=_cparams("parallel", "parallel"),
        name="od_in_ctx",
    )(ctx, scale, shift, *params)


def _od_lat_call(x, scale, shift, params, cos, sin, k_ctx, v_ctx, tm):
    nb, t, d = x.shape
    tc = k_ctx.shape[2]
    assert tc % tm == 0
    n_off = tc // tm
    lat = lambda i: jnp.maximum(i - n_off, 0)
    cxt = lambda i: jnp.minimum(i, n_off - 1)
    row = lambda b, i: (b, lat(i), 0)
    vec = lambda b, i: (b, 0, 0)
    head_q = lambda b, i: (b, 0, lat(i), 0)
    head_c = lambda b, i: (b, 0, cxt(i), 0)
    head_k = lambda b, i: (b, 0, i, 0)
    tab = lambda b, i: (lat(i), 0)
    rw = MLA_HEADS * QK_ROPE
    return pl.pallas_call(
        functools.partial(_od_lat_kernel, n_ctx_tiles=n_off),
        grid=(nb, n_off + t // tm),
        in_specs=[pl.BlockSpec((1, tm, d), row), pl.BlockSpec((1, 1, d), vec), pl.BlockSpec((1, 1, d), vec)]
                 + [_full(p.shape) for p in params]
                 + [pl.BlockSpec((tm, rw), tab), pl.BlockSpec((tm, rw), tab),
                    pl.BlockSpec((1, MLA_HEADS, tm, QK_DIM), head_c),
                    pl.BlockSpec((1, MLA_HEADS, tm, V_COLS), head_c)],
        out_specs=[pl.BlockSpec((1, MLA_HEADS, tm, QK_DIM), head_q),
                   pl.BlockSpec((1, MLA_HEADS, tm, QK_DIM), head_k),
                   pl.BlockSpec((1, MLA_HEADS, tm, V_COLS), head_k),
                   pl.BlockSpec((1, tm, MLA_WIDTH), row)],
        out_shape=[jax.ShapeDtypeStruct((nb, MLA_HEADS, t, QK_DIM), BF16),
                   jax.ShapeDtypeStruct((nb, MLA_HEADS, tc + t, QK_DIM), BF16),
                   jax.ShapeDtypeStruct((nb, MLA_HEADS, tc + t, V_COLS), BF16),
                   jax.ShapeDtypeStruct((nb, t, MLA_WIDTH), BF16)],
        compiler_params=_cparams("parallel", "arbitrary"),
        name="od_in_lat",
    )(x, scale, shift, *params, cos, sin, k_ctx, v_ctx)


ATTN_SUB = 512


def _attn_kernel(q_ref, k_ref, v_ref, o_ref):
    k = k_ref[0, 0]
    v = v_ref[0, 0]
    sub = min(ATTN_SUB, q_ref.shape[2])
    n_sub = q_ref.shape[2] // sub

    def scores(i):
        return lax.dot_general(q_ref[0, 0, i * sub:(i + 1) * sub, :], k, (((1,), (1,)), ((), ())),
                               preferred_element_type=F32)

    s = scores(0)
    for i in range(n_sub):
        s_next = scores(i + 1) if i + 1 < n_sub else None
        p = jnp.exp2((s - jnp.max(s, axis=-1, keepdims=True)).astype(BF16))
        ov = jnp.dot(p, v, preferred_element_type=F32)
        o_ref[0, i * sub:(i + 1) * sub, :] = (ov[:, :V_HEAD] / ov[:, V_HEAD:]).astype(BF16)
        s = s_next


def _attn_call(q, k, v, tq):
    nb, nh, t, _ = q.shape
    tk = k.shape[2]
    whole = lambda b, h, i: (b, h, 0, 0)
    return pl.pallas_call(
        _attn_kernel,
        grid=(nb, nh, t // tq),
        in_specs=[pl.BlockSpec((1, 1, tq, QK_DIM), lambda b, h, i: (b, h, i, 0)),
                  pl.BlockSpec((1, 1, tk, QK_DIM), whole),
                  pl.BlockSpec((1, 1, tk, V_COLS), whole)],
        out_specs=pl.BlockSpec((1, tq, V_HEAD), lambda b, h, i: (b, i, h)),
        out_shape=jax.ShapeDtypeStruct((nb, t, nh * V_HEAD), BF16),
        compiler_params=_cparams("parallel", "parallel", "parallel"),
        name="mla_attn",
    )(q, k, v)


def _od_out_kernel(o_ref, z_ref, wo_ref, x_ref, g_ref, out_ref):
    a = (o_ref[0].astype(F32) * _silu(z_ref[0].astype(F32))).astype(BF16)
    out_ref[0] = x_ref[0] + g_ref[0] * jnp.dot(a, wo_ref[...], preferred_element_type=F32)


def _od_out_call(o, z, wo_bf16, x, gate, tm):
    nb, t, d = x.shape
    row = lambda b, i: (b, i, 0)
    return pl.pallas_call(
        _od_out_kernel,
        grid=(nb, t // tm),
        in_specs=[pl.BlockSpec((1, tm, MLA_WIDTH), row), pl.BlockSpec((1, tm, MLA_WIDTH), row),
                  _full(wo_bf16.shape), pl.BlockSpec((1, tm, d), row),
                  pl.BlockSpec((1, 1, d), lambda b, i: (b, 0, 0))],
        out_specs=pl.BlockSpec((1, tm, d), row),
        out_shape=jax.ShapeDtypeStruct((nb, t, d), F32),
        compiler_params=_cparams("parallel", "parallel"),
        name="od_out",
    )(o, z, wo_bf16, x, gate)


def _rope_tables(n):
    rows = n // GRID_W
    row = jnp.repeat(jnp.arange(rows, dtype=F32), GRID_W)
    col = jnp.tile(jnp.arange(GRID_W, dtype=F32), rows)
    inv_freq = ROPE_THETA ** (-jnp.arange(ROPE_PAIRS, dtype=F32) / ROPE_PAIRS)
    ar = row[:, None] * inv_freq
    ac = col[:, None] * inv_freq
    cos = jnp.concatenate([jnp.cos(ar), jnp.cos(ar), jnp.cos(ac), jnp.cos(ac)], axis=1)
    sin = jnp.concatenate([-jnp.sin(ar), jnp.sin(ar), -jnp.sin(ac), jnp.sin(ac)], axis=1)
    return cos, sin


def _row_tile(t, want):
    tm = min(t, want)
    assert t % tm == 0 and tm % HALO == 0
    return tm


def kernel(x, c, ctx, c_ctx, ada_w, ada_b, norm_w, ev_w_in, ev_conv_w, ev_mu, ev_k_k, ev_k_a, ev_w0, ev_w2, ev_a0, ev_a2, ev_r_k, ev_lnx_w, ev_lnx_b, ev_w_out, od_w_in, od_q_a_norm, od_kv_a_norm, od_w_qb, od_w_kvb, od_gq_nope, od_gq_rope, od_gk_nope, od_gk_rope, od_w_o):
    nb, t, d = x.shape
    tc = ctx.shape[1]
    assert t % CHUNK == 0 and tc % CHUNK == 0 and t % GRID_W == 0

    rows = -(-(nb + 1) // HALO) * HALO
    cc = jnp.zeros((rows, d), F32).at[:nb].set(c).at[nb].set(c_ctx)
    mod = _ada_call(cc, ada_w, ada_b)

    def mods(layer):
        m = mod[layer]
        lat = [m[:nb, i * d:(i + 1) * d].reshape(nb, 1, d) for i in range(3)]
        cx = [jnp.broadcast_to(m[nb, i * d:(i + 1) * d].reshape(1, 1, d), (nb, 1, d)) for i in range(3)]
        return lat, cx

    e_np = np.arange(PAIR)[:, None] // RWKV_HEAD == np.arange(PAIR)[None, :] // RWKV_HEAD
    e_bf16 = jnp.asarray(e_np, BF16)
    ones_bf16 = jnp.ones((LANES, LANES), BF16)

    (sh_l, sc_l, g_l), (sh_c, sc_c, g_c) = mods(0)
    nw0 = norm_w[0].reshape(1, d)
    w_in0 = ev_w_in[0].astype(BF16)
    tm_l = _row_tile(t, 256)
    tm_c = _row_tile(tc, 256)
    pa_c, pr_c, zb_c = _ev_in_call(ctx, sc_c, sh_c, nw0, w_in0, tm_c)
    pa_l, pr_l, zb_l = _ev_in_call(x, sc_l, sh_l, nw0, w_in0, tm_l)

    W = RWKV_WIDTH
    vec = lambda a: a.reshape(1, -1)
    lora = jnp.zeros((2, 2 * LORA, 2 * W), F32)
    lora = lora.at[:, :LORA, :W].set(ev_w2[0]).at[:, LORA:, W:].set(ev_a2[0])
    lora_hi = lora.astype(BF16)
    lora_lo = (lora - lora_hi.astype(F32)).astype(BF16)
    wa0 = jnp.concatenate([ev_w0[0], ev_a0[0]], axis=-1).reshape(2, 1, 2 * W)
    params = [vec(1.0 - ev_mu[0]), vec(0.5 * ev_mu[0]), vec(ev_k_k[0]), vec(ev_k_a[0]), vec(1.0 - ev_k_a[0]),
              vec(ev_r_k[0]), wa0, lora_hi, lora_lo, e_bf16]
    yf_c, yb_c, bf_c, bb_c, s_c = _scan_call(pr_c, params, None)
    yf_l, yb_l, bf_l, bb_l, _ = _scan_call(pr_l, params, s_c)

    wo0 = ev_w_out[0].astype(BF16)
    lw, lb = vec(ev_lnx_w[0]), vec(ev_lnx_b[0])
    x1 = _ev_out_call(pa_l, zb_l, yf_l, yb_l, bf_l, bb_l, ev_conv_w[0], lw, lb, e_bf16, wo0, x, g_l,
                      _row_tile(t, 512))
    ctx1 = _ev_out_call(pa_c, zb_c, yf_c, yb_c, bf_c, bb_c, ev_conv_w[0], lw, lb, e_bf16, wo0, ctx, g_c, tm_c)

    (sh_l, sc_l, g_l), (sh_c, sc_c, _) = mods(1)
    nw1 = norm_w[1].reshape(1, d)
    swap = np.arange(QK_ROPE) ^ ROPE_PAIRS
    hq = np.arange(MLA_HEADS)[:, None] * QK_DIM
    idx_q = np.concatenate([(hq + np.arange(QK_NOPE)).ravel(),
                            (hq + QK_NOPE + np.arange(QK_ROPE)).ravel(),
                            (hq + QK_NOPE + swap).ravel()])
    hk = np.arange(MLA_HEADS)[:, None] * (QK_NOPE + V_HEAD)
    idx_kv = np.concatenate([(hk + np.arange(QK_NOPE)).ravel(), (hk + QK_NOPE + np.arange(V_HEAD)).ravel()])
    r_off = Q_LORA + KV_LORA
    idx_in = np.concatenate([np.arange(r_off + QK_ROPE), r_off + swap,
                             np.arange(r_off + QK_ROPE, od_w_in.shape[2])])
    w_in1 = od_w_in[0][:, idx_in].astype(BF16)
    wqb = od_w_qb[0][:, idx_q].astype(BF16)
    wkvb = od_w_kvb[0][:, idx_kv].astype(BF16)
    gqr = jnp.tile(od_gq_rope[0], MLA_HEADS).reshape(1, -1)
    gqrs = jnp.tile(od_gq_rope[0][swap], MLA_HEADS).reshape(1, -1)
    gkr, gkrs = vec(od_gk_rope[0]), vec(od_gk_rope[0][swap])
    gqn, gkn = vec(od_gq_nope[0]), vec(od_gk_nope[0])
    qan, kvn = vec(od_q_a_norm[0]), vec(od_kv_a_norm[0])
    cos, sin = _rope_tables(t)
    cos = jnp.tile(cos, (1, MLA_HEADS))
    sin = jnp.tile(sin, (1, MLA_HEADS))

    k_c, v_c = _od_ctx_call(ctx1, sc_c, sh_c, nw1, w_in1[:, Q_LORA:Q_LORA + KV_LORA + 2 * QK_ROPE],
                            kvn, wkvb, gkn, gkr, gkrs, ones_bf16, _row_tile(tc, 256))
    params = [nw1, w_in1, qan, kvn, wqb, wkvb, gqn, gqr, gqrs, gkn, gkr, gkrs, ones_bf16, e_bf16]
    q, k_all, v_all, z = _od_lat_call(x1, sc_l, sh_l, params, cos, sin, k_c, v_c, _row_tile(tc, 256))
    o = _attn_call(q, k_all, v_all, _row_tile(t, 4 * ATTN_SUB))
    return _od_out_call(o, z, od_w_o[0].astype(BF16), x1, g_l, _row_tile(t, 512))
```
